```python
import jax, jax.numpy as jnp
from jax import lax
import numpy as np

D_MODEL = 1024
BATCH = 8
SEQ = 4096
DEPTH = 2

CTX_LEN = 256
GRID_W = 64
HEAD_DIM = 64
D_FF = ((8 * D_MODEL // 3 + 127) // 128) * 128
D_LRU = D_MODEL // 2
N_LRU_HEADS = D_LRU // HEAD_DIM
D_GMLP = D_MODEL // 4
N_GMLP_GROUPS = D_GMLP // HEAD_DIM
GMLP_CHUNK = 128
D_FNET = D_MODEL // 4
N_FNET_GROUPS = D_FNET // HEAD_DIM
D_MIX = D_LRU + D_GMLP + D_FNET
D_IN = 2 * D_LRU + 2 * D_GMLP + D_FNET
CONV_W = 4
LRU_C = 8.0
N_MOD = 9
EPS = 1e-6

kernel_name = "hybrid_rglru_gmlp_fnet_macaron_dit"


def rmsnorm(x, g):
    xf = x.astype(jnp.float32)
    y = xf * lax.rsqrt(jnp.mean(xf * xf, axis=-1, keepdims=True) + EPS)
    return (y * g.astype(jnp.float32)).astype(x.dtype)


def adaln(c_vec, w, b):
    m = jax.nn.silu(c_vec) @ w + b
    return m.reshape(m.shape[:-1] + (1, N_MOD, D_MODEL))


def sincos_2d(rows, d):
    r, col = jnp.meshgrid(jnp.arange(rows, dtype=jnp.float32),
                          jnp.arange(GRID_W, dtype=jnp.float32), indexing="ij")
    q = d // 4
    freqs = 1.0 / (10000.0 ** (jnp.arange(q, dtype=jnp.float32) / q))
    er = r.reshape(-1, 1) * freqs
    ec = col.reshape(-1, 1) * freqs
    return jnp.concatenate([jnp.sin(er), jnp.cos(er), jnp.sin(ec), jnp.cos(ec)], axis=-1)


def ffn_sublayer(h, mod, k0, g, w_gu, w_down):
    shift, scale, gate = mod[..., k0, :], mod[..., k0 + 1, :], mod[..., k0 + 2, :]
    n = rmsnorm(h, g) * (1 + scale) + shift
    gt, up = jnp.split(n @ w_gu, 2, axis=-1)
    return h + 0.5 * gate * ((jax.nn.silu(gt) * up) @ w_down)


def mix_in(h, mod, g, w_in):
    n = rmsnorm(h, g) * (1 + mod[..., 4, :]) + mod[..., 3, :]
    z = n @ w_in
    return jnp.split(z, [D_LRU, 2 * D_LRU, 2 * D_LRU + D_GMLP, 2 * D_LRU + 2 * D_GMLP], axis=-1)


def centred_conv(x, w, b):
    L = x.shape[1]
    left = CONV_W // 2
    right = CONV_W - 1 - left
    xp = jnp.pad(x, ((0, 0), (left, right), (0, 0)))
    return b + sum(xp[:, k:k + L] * w[k] for k in range(CONV_W))


def _lin_combine(e1, e2):
    a1, b1 = e1
    a2, b2 = e2
    return a1 * a2, a2 * b1 + b2


def rglru_direction(xc, w_g, b_g, lam, h0, reverse):
    B_, L, _ = xc.shape
    xh = xc.reshape(B_, L, N_LRU_HEADS, HEAD_DIM)
    gates = jnp.einsum("blhi,ghij->gblhj", xh, w_g).reshape(2, B_, L, D_LRU) + b_g[:, None, None, :]
    gates = gates.astype(jnp.float32)
    r = jax.nn.sigmoid(gates[0])
    i = jax.nn.sigmoid(gates[1])
    log_a = -LRU_C * r * jax.nn.softplus(-lam.astype(jnp.float32))
    a = jnp.exp(log_a)
    b = jnp.sqrt(-jnp.expm1(2.0 * log_a)) * i * xc.astype(jnp.float32)
    if h0 is not None:
        edge = -1 if reverse else 0
        b = b.at[:, edge].add(a[:, edge] * h0)
    _, h = lax.associative_scan(_lin_combine, (a, b), axis=1, reverse=reverse)
    final = h[:, 0] if reverse else h[:, -1]
    return h, final


def lru_branch(xa, conv_w, conv_b, w_gates, b_gates, lam, h0):
    xc = centred_conv(xa, conv_w, conv_b)
    hf, sf = rglru_direction(xc, w_gates[0], b_gates[0], lam[0], None if h0 is None else h0[0], False)
    hb, sb = rglru_direction(xc, w_gates[1], b_gates[1], lam[1], None if h0 is None else h0[1], True)
    return hf + hb, jnp.stack([sf, sb])


def gmlp_chunk(u, v, ws, bs):
    B_, L, _ = v.shape
    vc = v.reshape(B_, L // GMLP_CHUNK, GMLP_CHUNK, N_GMLP_GROUPS, D_GMLP // N_GMLP_GROUPS)
    mixed = jnp.einsum("gpq,bcqgd->bcpgd", ws, vc) + bs.T[None, None, :, :, None]
    return u * mixed.reshape(B_, L, D_GMLP)


def fourier_mix(f):
    B_, L, _ = f.shape
    fg = f.reshape(B_, L, N_FNET_GROUPS, D_FNET // N_FNET_GROUPS).astype(jnp.float32)
    y = jnp.fft.fft2(fg, axes=(1, 3), norm="ortho").real
    return y.reshape(B_, L, D_FNET).astype(f.dtype)


def mix_out(h, mod, y_lru, ga, u, v, f, ws, bs, w_out):
    y_a = y_lru.astype(h.dtype) * jax.nn.gelu(ga)
    y_b = gmlp_chunk(jax.nn.gelu(u), jax.nn.gelu(v), ws, bs)
    y_c = fourier_mix(f)
    y = jnp.concatenate([y_a, y_b, y_c], axis=-1) @ w_out
    return h + mod[..., 5, :] * y


def setup_inputs(seed: int = 0) -> dict:
    key = jax.random.key(seed)
    ks = jax.random.split(key, 20)

    def nrm(k, shape, scale):
        return scale * jax.random.normal(k, shape, jnp.float32)

    x = nrm(ks[0], (BATCH, SEQ, D_MODEL), 1.0)
    c = nrm(ks[1], (BATCH, D_MODEL), 1.0)
    ctx = nrm(ks[2], (BATCH, CTX_LEN, D_MODEL), 1.0)
    c_ctx = nrm(ks[3], (D_MODEL,), 1.0)
    w_mod = nrm(ks[4], (DEPTH, D_MODEL, N_MOD * D_MODEL), 0.5 * D_MODEL ** -0.5)
    b_mod = nrm(ks[5], (DEPTH, N_MOD * D_MODEL), 0.02)
    norm_g = 1.0 + nrm(ks[6], (DEPTH, 3, D_MODEL), 0.02)
    ffn_w_gu = nrm(ks[7], (DEPTH, 2, D_MODEL, 2 * D_FF), D_MODEL ** -0.5)
    ffn_w_down = nrm(ks[8], (DEPTH, 2, D_FF, D_MODEL), D_FF ** -0.5)
    w_in = nrm(ks[9], (DEPTH, D_MODEL, D_IN), D_MODEL ** -0.5)
    w_out = nrm(ks[10], (DEPTH, D_MIX, D_MODEL), D_MIX ** -0.5)
    conv_w = nrm(ks[11], (DEPTH, CONV_W, D_LRU), CONV_W ** -0.5)
    conv_b = nrm(ks[12], (DEPTH, D_LRU), 0.02)
    lru_w_gates = nrm(ks[13], (DEPTH, 2, 2, N_LRU_HEADS, HEAD_DIM, HEAD_DIM), HEAD_DIM ** -0.5)
    lru_b_gates = nrm(ks[14], (DEPTH, 2, 2, D_LRU), 0.02)
    a_c = jax.random.uniform(ks[15], (DEPTH, 2, D_LRU), jnp.float32, 0.9, 0.999)
    a = a_c ** (1.0 / LRU_C)
    lru_lambda = jnp.log(a) - jnp.log1p(-a)
    gmlp_ws = nrm(ks[16], (DEPTH, N_GMLP_GROUPS, GMLP_CHUNK, GMLP_CHUNK), GMLP_CHUNK ** -0.5)
    gmlp_bs = 1.0 + nrm(ks[17], (DEPTH, N_GMLP_GROUPS, GMLP_CHUNK), 0.02)
    final_norm_g = 1.0 + nrm(ks[18], (D_MODEL,), 0.02)
    return {"x": x, "c": c, "ctx": ctx, "c_ctx": c_ctx, "w_mod": w_mod, "b_mod": b_mod,
            "norm_g": norm_g, "ffn_w_gu": ffn_w_gu, "ffn_w_down": ffn_w_down,
            "w_in": w_in, "w_out": w_out, "conv_w": conv_w, "conv_b": conv_b,
            "lru_w_gates": lru_w_gates, "lru_b_gates": lru_b_gates, "lru_lambda": lru_lambda,
            "gmlp_ws": gmlp_ws, "gmlp_bs": gmlp_bs, "final_norm_g": final_norm_g}


def reference(x, c, ctx, c_ctx, w_mod, b_mod, norm_g, ffn_w_gu, ffn_w_down, w_in, w_out,
              conv_w, conv_b, lru_w_gates, lru_b_gates, lru_lambda, gmlp_ws, gmlp_bs,
              final_norm_g):
    rows = x.shape[1] // GRID_W
    h = x + sincos_2d(rows, D_MODEL).astype(x.dtype)[None]
    hc = ctx
    for l in range(DEPTH):
        last = l == DEPTH - 1
        mod = adaln(c, w_mod[l], b_mod[l])
        mod_c = adaln(c_ctx, w_mod[l], b_mod[l])

        h = ffn_sublayer(h, mod, 0, norm_g[l, 0], ffn_w_gu[l, 0], ffn_w_down[l, 0])
        hc = ffn_sublayer(hc, mod_c, 0, norm_g[l, 0], ffn_w_gu[l, 0], ffn_w_down[l, 0])

        xa_c, ga_c, u_c, v_c, f_c = mix_in(hc, mod_c, norm_g[l, 1], w_in[l])
        y_lru_c, state_c = lru_branch(xa_c, conv_w[l], conv_b[l], lru_w_gates[l],
                                      lru_b_gates[l], lru_lambda[l], None)
        xa, ga, u, v, f = mix_in(h, mod, norm_g[l, 1], w_in[l])
        y_lru, _ = lru_branch(xa, conv_w[l], conv_b[l], lru_w_gates[l],
                              lru_b_gates[l], lru_lambda[l], state_c)
        h = mix_out(h, mod, y_lru, ga, u, v, f, gmlp_ws[l], gmlp_bs[l], w_out[l])

        h = ffn_sublayer(h, mod, 6, norm_g[l, 2], ffn_w_gu[l, 1], ffn_w_down[l, 1])
        if not last:
            hc = mix_out(hc, mod_c, y_lru_c, ga_c, u_c, v_c, f_c, gmlp_ws[l], gmlp_bs[l], w_out[l])
            hc = ffn_sublayer(hc, mod_c, 6, norm_g[l, 2], ffn_w_gu[l, 1], ffn_w_down[l, 1])
    return rmsnorm(h, final_norm_g)
```

```python
import functools
import math

import jax
import jax.numpy as jnp
from jax import lax
from jax.experimental import pallas as pl
from jax.experimental.pallas import tpu as pltpu

D_MODEL = 1024
GRID_W = 64
HEAD_DIM = 64
D_FF = ((8 * D_MODEL // 3 + 127) // 128) * 128
D_LRU = D_MODEL // 2
N_LRU_HEADS = D_LRU // HEAD_DIM
D_GMLP = D_MODEL // 4
N_GMLP_GROUPS = D_GMLP // HEAD_DIM
GMLP_CHUNK = 128
D_FNET = D_MODEL // 4
N_FNET_GROUPS = D_FNET // HEAD_DIM
D_IN = 2 * D_LRU + 2 * D_GMLP + D_FNET
CONV_W = 4
LRU_C = 8.0
N_MOD = 9
EPS = 1e-6

LANES = 128
SUBLANES = 8
VMEM_LIMIT_BYTES = 56 * 1024 * 1024

ROW_TILE = 512
FF_CHUNK = 256
MOD_ROWS = 16
LRU_GROUP = LANES
N_LRU_GROUPS = D_LRU // LRU_GROUP
N_SEG = SUBLANES

BF16 = jnp.bfloat16
F32 = jnp.float32


def _cparams(sem):
    return pltpu.CompilerParams(dimension_semantics=sem, vmem_limit_bytes=VMEM_LIMIT_BYTES)


def _resident(shape, index_map):
    return pl.BlockSpec(shape, index_map, pipeline_mode=pl.Buffered(1))


def _adaln_kernel(c_ref, w_ref, b_ref, o_ref):
    c = c_ref[...]
    s = (c * jax.nn.sigmoid(c)).astype(BF16)
    o_ref[...] = jnp.dot(s, w_ref[...].astype(BF16), preferred_element_type=F32) + b_ref[...]


def _adaln(cvec, w_mod, b_mod):
    depth = w_mod.shape[0]
    return pl.pallas_call(
        _adaln_kernel,
        grid=(depth, N_MOD),
        in_specs=[
            pl.BlockSpec((MOD_ROWS, D_MODEL), lambda l, j: (0, 0)),
            pl.BlockSpec((None, D_MODEL, D_MODEL), lambda l, j: (l, 0, j)),
            pl.BlockSpec((None, 1, D_MODEL), lambda l, j: (l, 0, j)),
        ],
        out_specs=pl.BlockSpec((None, MOD_ROWS, D_MODEL), lambda l, j: (l, 0, j)),
        out_shape=jax.ShapeDtypeStruct((depth, MOD_ROWS, N_MOD * D_MODEL), F32),
        compiler_params=_cparams(("arbitrary", "arbitrary")),
        name="adaln",
    )(cvec, w_mod, b_mod.reshape(depth, 1, N_MOD * D_MODEL))


def _modulated_norm(h, g, shift, scale):
    y = h * lax.rsqrt(jnp.mean(h * h, axis=-1, keepdims=True) + EPS)
    return ((y * g) * (1.0 + scale) + shift).astype(BF16)


def _mod_index(tiles_per_seq, fixed_row):
    if fixed_row is not None:
        return lambda i: (fixed_row, 0, 0)
    return lambda i: (i // tiles_per_seq, 0, 0)


def _ffn_kernel(*refs, k0, embed, final_norm):
    if embed:
        h_ref, tab_ref, mod_ref, g_ref, wgu_ref, wd_ref = refs[:6]
        rest = refs[6:]
    else:
        h_ref, mod_ref, g_ref, wgu_ref, wd_ref = refs[:5]
        rest = refs[5:]
    if final_norm:
        gf_ref, o_ref = rest
    else:
        (o_ref,) = rest

    h = h_ref[...]
    if embed:
        tm = h.shape[0]
        n_grid_rows = tm // GRID_W
        tiles_per_seq = pl.num_programs(0) // embed
        r0 = (pl.program_id(0) % tiles_per_seq) * n_grid_rows
        tab = tab_ref[...]
        row_part = jnp.concatenate(
            [jnp.broadcast_to(tab_ref[pl.ds(r0 + a, 1), :], (GRID_W, D_MODEL // 2))
             for a in range(n_grid_rows)], axis=0)
        col_part = jnp.concatenate([tab] * n_grid_rows, axis=0)
        h = h + jnp.concatenate([row_part, col_part], axis=1)

    shift = mod_ref[k0:k0 + 1, :]
    scale = mod_ref[k0 + 1:k0 + 2, :]
    gate = mod_ref[k0 + 2:k0 + 3, :]
    n = _modulated_norm(h, g_ref[...], shift, scale)

    acc = jnp.zeros(h.shape, F32)
    for j in range(D_FF // FF_CHUNK):
        lo = j * FF_CHUNK
        gt = jnp.dot(n, wgu_ref[:, lo:lo + FF_CHUNK], preferred_element_type=F32)
        up = jnp.dot(n, wgu_ref[:, D_FF + lo:D_FF + lo + FF_CHUNK], preferred_element_type=F32)
        a = (gt * jax.nn.sigmoid(gt) * up).astype(BF16)
        acc = acc + jnp.dot(a, wd_ref[lo:lo + FF_CHUNK, :], preferred_element_type=F32)
    out = h + 0.5 * gate * acc
    if final_norm:
        out = out * lax.rsqrt(jnp.mean(out * out, axis=-1, keepdims=True) + EPS) * gf_ref[...]
    o_ref[...] = out


def _ffn(h, mod, g, wgu, wd, *, k0, tiles_per_seq=None, fixed_row=None, pos_tab=None, batch=None,
         final_g=None):
    rows = h.shape[0]
    n_tiles = rows // ROW_TILE
    row_spec = pl.BlockSpec((ROW_TILE, D_MODEL), lambda i: (i, 0))
    in_specs = [row_spec]
    args = [h]
    if pos_tab is not None:
        in_specs.append(_resident((GRID_W, D_MODEL // 2), lambda i: (0, 0)))
        args.append(pos_tab)
    in_specs += [
        pl.BlockSpec((None, N_MOD, D_MODEL), _mod_index(tiles_per_seq, fixed_row)),
        _resident((1, D_MODEL), lambda i: (0, 0)),
        _resident((D_MODEL, 2 * D_FF), lambda i: (0, 0)),
        _resident((D_FF, D_MODEL), lambda i: (0, 0)),
    ]
    args += [mod, g.reshape(1, D_MODEL), wgu, wd]
    if final_g is not None:
        in_specs.append(_resident((1, D_MODEL), lambda i: (0, 0)))
        args.append(final_g.reshape(1, D_MODEL))
    kern = functools.partial(_ffn_kernel, k0=k0, embed=batch if pos_tab is not None else 0,
                             final_norm=final_g is not None)
    return pl.pallas_call(
        kern,
        grid=(n_tiles,),
        in_specs=in_specs,
        out_specs=row_spec,
        out_shape=jax.ShapeDtypeStruct((rows, D_MODEL), F32),
        compiler_params=_cparams(("arbitrary",)),
        name="ffn",
    )(*args)


def _mix_in_kernel(h_ref, mod_ref, g_ref, w_ref, cs_ref, xa_ref, ga_ref, gu_ref, gv_ref, gf_ref):
    n = _modulated_norm(h_ref[...], g_ref[...], mod_ref[3:4, :], mod_ref[4:5, :])

    def proj(lo, width):
        return jnp.dot(n, w_ref[:, lo:lo + width], preferred_element_type=F32)

    xa_ref[...] = proj(0, D_LRU)
    ga_ref[...] = jax.nn.gelu(proj(D_LRU, D_LRU))
    gu_ref[...] = jax.nn.gelu(proj(2 * D_LRU, D_GMLP))
    gv_ref[...] = jax.nn.gelu(proj(2 * D_LRU + D_GMLP, D_GMLP)).astype(BF16)
    f = proj(2 * D_LRU + 2 * D_GMLP, D_FNET).astype(BF16)
    gf_ref[...] = jnp.dot(f, cs_ref[...], preferred_element_type=F32).astype(BF16)


def _mix_in(h, mod, g, w_in, cs, *, tiles_per_seq=None, fixed_row=None):
    rows = h.shape[0]
    row = lambda w: pl.BlockSpec((ROW_TILE, w), lambda i: (i, 0))
    return pl.pallas_call(
        _mix_in_kernel,
        grid=(rows // ROW_TILE,),
        in_specs=[
            row(D_MODEL),
            pl.BlockSpec((None, N_MOD, D_MODEL), _mod_index(tiles_per_seq, fixed_row)),
            _resident((1, D_MODEL), lambda i: (0, 0)),
            _resident((D_MODEL, D_IN), lambda i: (0, 0)),
            _resident((D_FNET, 2 * D_FNET), lambda i: (0, 0)),
        ],
        out_specs=[row(D_LRU), row(D_LRU), row(D_GMLP), row(D_GMLP), row(2 * D_FNET)],
        out_shape=[
            jax.ShapeDtypeStruct((rows, D_LRU), F32),
            jax.ShapeDtypeStruct((rows, D_LRU), F32),
            jax.ShapeDtypeStruct((rows, D_GMLP), F32),
            jax.ShapeDtypeStruct((rows, D_GMLP), BF16),
            jax.ShapeDtypeStruct((rows, 2 * D_FNET), BF16),
        ],
        compiler_params=_cparams(("arbitrary",)),
        name="mix_in",
    )(h, mod, g.reshape(1, D_MODEL), w_in, cs)


def _lru_kernel(xa_ref, ga_ref, cw_ref, cb_ref, wg_ref, bg_ref, lam_ref, h0_ref,
                ya_ref, st_ref, xpad_ref, af_ref, bf_ref, ab_ref, bb_ref, y_ref, *, seq):
    seg = seq // N_SEG
    pad = SUBLANES

    zeros = jnp.zeros((pad, LRU_GROUP), F32)
    xpad_ref[0:pad, :] = zeros
    xpad_ref[pad + seq:pad + seq + pad, :] = zeros
    xpad_ref[pad:pad + seq, :] = xa_ref[...]

    lam = lam_ref[...]
    neg_c_softplus = -LRU_C * jax.nn.softplus(-lam)
    cw = cw_ref[...]
    cb = cb_ref[...]
    bg = bg_ref[...]
    dir_refs = ((af_ref, bf_ref), (ab_ref, bb_ref))

    for j in range(N_SEG):
        base = pad + j * seg
        xc = cb
        for k in range(CONV_W):
            xc = xc + xpad_ref[base + k - CONV_W // 2:base + k - CONV_W // 2 + seg, :] * cw[k:k + 1, :]
        gates = jnp.dot(xc.astype(BF16), wg_ref[...], preferred_element_type=F32) + bg
        for d in range(2):
            r = jax.nn.sigmoid(gates[:, (2 * d) * LRU_GROUP:(2 * d + 1) * LRU_GROUP])
            i = jax.nn.sigmoid(gates[:, (2 * d + 1) * LRU_GROUP:(2 * d + 2) * LRU_GROUP])
            log_a = neg_c_softplus[d:d + 1, :] * r
            a = jnp.exp(log_a)
            b = jnp.sqrt(-jnp.tanh(log_a) * (a * a + 1.0)) * i * xc
            a_ref, b_ref = dir_refs[d]
            a_ref[pl.ds(j, seg, stride=N_SEG), :] = a
            b_ref[pl.ds(j, seg, stride=N_SEG), :] = b

    def scan_step(t, carry):
        pf, hf, pb, hb = carry
        rf = pl.multiple_of(t * N_SEG, N_SEG)
        rb = pl.multiple_of((seg - 1 - t) * N_SEG, N_SEG)
        a = af_ref[pl.ds(rf, N_SEG), :]
        pf = a * pf
        hf = a * hf + bf_ref[pl.ds(rf, N_SEG), :]
        af_ref[pl.ds(rf, N_SEG), :] = pf
        bf_ref[pl.ds(rf, N_SEG), :] = hf
        a = ab_ref[pl.ds(rb, N_SEG), :]
        pb = a * pb
        hb = a * hb + bb_ref[pl.ds(rb, N_SEG), :]
        ab_ref[pl.ds(rb, N_SEG), :] = pb
        bb_ref[pl.ds(rb, N_SEG), :] = hb
        return pf, hf, pb, hb

    one = jnp.ones((N_SEG, LRU_GROUP), F32)
    zero = jnp.zeros((N_SEG, LRU_GROUP), F32)
    pf, hf, pb, hb = lax.fori_loop(0, seg, scan_step, (one, zero, one, zero), unroll=8)

    h0 = h0_ref[...]
    c = h0[0:1, :]
    rows = []
    for j in range(N_SEG):
        rows.append(c)
        c = hf[j:j + 1, :] + pf[j:j + 1, :] * c
    carry_f = jnp.concatenate(rows, axis=0)
    final_f = c
    c = h0[1:2, :]
    rows = []
    for j in reversed(range(N_SEG)):
        rows.append(c)
        c = hb[j:j + 1, :] + pb[j:j + 1, :] * c
    carry_b = jnp.concatenate(rows[::-1], axis=0)
    final_b = c
    st_ref[...] = jnp.concatenate([final_f, final_b], axis=0)

    def fix_step(t, _):
        r = pl.multiple_of(t * N_SEG, N_SEG)
        y_ref[pl.ds(r, N_SEG), :] = (
            (bf_ref[pl.ds(r, N_SEG), :] + af_ref[pl.ds(r, N_SEG), :] * carry_f)
            + (bb_ref[pl.ds(r, N_SEG), :] + ab_ref[pl.ds(r, N_SEG), :] * carry_b))
        return 0

    lax.fori_loop(0, seg, fix_step, 0, unroll=8)

    for j in range(N_SEG):
        y = y_ref[pl.ds(j, seg, stride=N_SEG), :]
        ya_ref[j * seg:(j + 1) * seg, :] = (y * ga_ref[j * seg:(j + 1) * seg, :]).astype(BF16)


def _lru(xa, ga, conv_w, conv_b, wg, bg, lam, h0, *, batch, seq):
    blk = lambda: pl.BlockSpec((seq, LRU_GROUP), lambda b, g: (b, g))
    grp = lambda r: pl.BlockSpec((r, LRU_GROUP), lambda b, g: (0, g))
    st = lambda: pl.BlockSpec((None, 2, LRU_GROUP), lambda b, g: (b, 0, g))
    return pl.pallas_call(
        functools.partial(_lru_kernel, seq=seq),
        grid=(batch, N_LRU_GROUPS),
        in_specs=[
            blk(), blk(), grp(CONV_W), grp(1),
            pl.BlockSpec((None, LRU_GROUP, 4 * LRU_GROUP), lambda b, g: (g, 0, 0)),
            pl.BlockSpec((None, 1, 4 * LRU_GROUP), lambda b, g: (g, 0, 0)),
            grp(2), st(),
        ],
        out_specs=[blk(), st()],
        out_shape=[
            jax.ShapeDtypeStruct((batch * seq, D_LRU), BF16),
            jax.ShapeDtypeStruct((batch, 2, D_LRU), F32),
        ],
        scratch_shapes=[pltpu.VMEM((seq + 2 * SUBLANES, LRU_GROUP), F32)]
        + [pltpu.VMEM((seq, LRU_GROUP), F32) for _ in range(5)],
        compiler_params=_cparams(("arbitrary", "arbitrary")),
        name="lru",
    )(xa, ga, conv_w, conv_b.reshape(1, D_LRU), wg, bg, lam, h0)


def _lru_gate_weights(w_gates, b_gates):
    heads_per_group = LRU_GROUP // HEAD_DIM
    w = w_gates.reshape(2, 2, N_LRU_GROUPS, heads_per_group, HEAD_DIM, HEAD_DIM)
    eye = jnp.eye(heads_per_group, dtype=w.dtype)
    bd = w[:, :, :, :, :, None, :] * eye[None, None, None, :, None, :, None]
    bd = bd.reshape(2, 2, N_LRU_GROUPS, LRU_GROUP, LRU_GROUP)
    wg = jnp.transpose(bd, (2, 3, 0, 1, 4)).reshape(N_LRU_GROUPS, LRU_GROUP, 4 * LRU_GROUP)
    b = b_gates.reshape(2, 2, N_LRU_GROUPS, LRU_GROUP)
    bg = jnp.transpose(b, (2, 0, 1, 3)).reshape(N_LRU_GROUPS, 1, 4 * LRU_GROUP)
    return wg.astype(BF16), bg


def _dft_kernel(dm_ref, g_ref, o_ref, acc_ref):
    kk = pl.program_id(1)

    @pl.when(kk == 0)
    def _():
        acc_ref[...] = jnp.zeros_like(acc_ref)

    dm = dm_ref[...]
    for b in range(g_ref.shape[0]):
        acc_ref[b] += jnp.dot(dm, g_ref[b], preferred_element_type=F32)

    @pl.when(kk == pl.num_programs(1) - 1)
    def _():
        o_ref[...] = acc_ref[...].astype(BF16)


def _dft(dm, gf, *, batch, seq):
    tm = min(seq, 1024)
    tk = min(seq, 1024)
    nk = seq // tk
    return pl.pallas_call(
        _dft_kernel,
        grid=(seq // tm, 2 * nk),
        in_specs=[
            pl.BlockSpec((tm, tk), lambda i, k: (i, k)),
            pl.BlockSpec((batch, tk, D_FNET), lambda i, k: (0, k % nk, k // nk)),
        ],
        out_specs=pl.BlockSpec((batch, tm, D_FNET), lambda i, k: (0, i, 0)),
        out_shape=jax.ShapeDtypeStruct((batch, seq, D_FNET), BF16),
        scratch_shapes=[pltpu.VMEM((batch, tm, D_FNET), F32)],
        compiler_params=_cparams(("arbitrary", "arbitrary")),
        name="dft",
    )(dm, gf.reshape(batch, seq, 2 * D_FNET))


def _dft_matrix(seq):
    scale = 1.0 / math.sqrt(seq * HEAD_DIM)
    n = jnp.arange(seq, dtype=jnp.int32)
    ang = ((n[:, None] * n[None, :]) % seq).astype(F32) * (2.0 * math.pi / seq)
    return jnp.concatenate([jnp.cos(ang) * scale, -jnp.sin(ang) * scale], axis=1).astype(BF16)


def _channel_dft_matrix():
    c = jnp.arange(D_FNET, dtype=jnp.int32)
    same_group = (c[:, None] // HEAD_DIM) == (c[None, :] // HEAD_DIM)
    ang = (((c[:, None] % HEAD_DIM) * (c[None, :] % HEAD_DIM)) % HEAD_DIM).astype(F32) * (
        2.0 * math.pi / HEAD_DIM)
    cs = jnp.concatenate([jnp.where(same_group, jnp.cos(ang), 0.0),
                          jnp.where(same_group, jnp.sin(ang), 0.0)], axis=1)
    return cs.astype(BF16)


def _mix_out_kernel(h_ref, mod_ref, ya_ref, gu_ref, gv_ref, yc_ref, ws_ref, bs_ref, wo_ref, o_ref):
    tm = h_ref.shape[0]
    half = D_GMLP // 2
    lane = lax.broadcasted_iota(jnp.int32, (GMLP_CHUNK, half), 1)
    first_group = lane < HEAD_DIM
    yb = []
    for c in range(tm // GMLP_CHUNK):
        rows = slice(c * GMLP_CHUNK, (c + 1) * GMLP_CHUNK)
        cols = []
        for p in range(2):
            m = jnp.dot(ws_ref[p * 2 * GMLP_CHUNK:(p + 1) * 2 * GMLP_CHUNK, :],
                        gv_ref[rows, p * half:(p + 1) * half], preferred_element_type=F32)
            cols.append(jnp.where(first_group, m[:GMLP_CHUNK], m[GMLP_CHUNK:]))
        mixed = jnp.concatenate(cols, axis=1) + bs_ref[...]
        yb.append((gu_ref[rows, :] * mixed).astype(BF16))
    yb = jnp.concatenate(yb, axis=0)
    y = jnp.dot(ya_ref[...], wo_ref[0:D_LRU, :], preferred_element_type=F32)
    y = y + jnp.dot(yb, wo_ref[D_LRU:D_LRU + D_GMLP, :], preferred_element_type=F32)
    y = y + jnp.dot(yc_ref[...], wo_ref[D_LRU + D_GMLP:, :], preferred_element_type=F32)
    o_ref[...] = h_ref[...] + mod_ref[5:6, :] * y


def _mix_out(h, mod, ya, gu, gv, yc, ws, bs, w_out, *, tiles_per_seq=None, fixed_row=None):
    rows = h.shape[0]
    row = lambda w: pl.BlockSpec((ROW_TILE, w), lambda i: (i, 0))
    return pl.pallas_call(
        _mix_out_kernel,
        grid=(rows // ROW_TILE,),
        in_specs=[
            row(D_MODEL),
            pl.BlockSpec((None, N_MOD, D_MODEL), _mod_index(tiles_per_seq, fixed_row)),
            row(D_LRU), row(D_GMLP), row(D_GMLP), row(D_FNET),
            _resident((N_GMLP_GROUPS * GMLP_CHUNK, GMLP_CHUNK), lambda i: (0, 0)),
            _resident((GMLP_CHUNK, D_GMLP), lambda i: (0, 0)),
            _resident((D_MODEL, D_MODEL), lambda i: (0, 0)),
        ],
        out_specs=row(D_MODEL),
        out_shape=jax.ShapeDtypeStruct((rows, D_MODEL), F32),
        compiler_params=_cparams(("arbitrary",)),
        name="mix_out",
    )(h, mod, ya, gu, gv, yc, ws, bs, w_out)


def _pos_table_kernel(f_ref, o_ref):
    j = lax.broadcasted_iota(jnp.int32, (GRID_W, D_MODEL // 4), 0).astype(F32)
    e = j * f_ref[...]
    o_ref[...] = jnp.concatenate([jnp.sin(e), jnp.cos(e)], axis=1)


def _pos_table():
    q = D_MODEL // 4
    freqs = 1.0 / (10000.0 ** (jnp.arange(q, dtype=F32) / q))
    return pl.pallas_call(
        _pos_table_kernel,
        out_shape=jax.ShapeDtypeStruct((GRID_W, D_MODEL // 2), F32),
        name="pos_table",
    )(freqs.reshape(1, q))


def kernel(x, c, ctx, c_ctx, w_mod, b_mod, norm_g, ffn_w_gu, ffn_w_down, w_in, w_out, conv_w, conv_b,
           lru_w_gates, lru_b_gates, lru_lambda, gmlp_ws, gmlp_bs, final_norm_g):
    batch, seq, _ = x.shape
    ctx_len = ctx.shape[1]
    depth = w_mod.shape[0]
    assert batch < MOD_ROWS and seq % ROW_TILE == 0 and (batch * ctx_len) % ROW_TILE == 0
    assert ROW_TILE % ctx_len == 0 or ctx_len % ROW_TILE == 0
    assert seq % (N_SEG * SUBLANES) == 0 and ctx_len % (N_SEG * SUBLANES) == 0
    ctx_row = batch
    tiles_per_seq = seq // ROW_TILE

    cvec = jnp.zeros((MOD_ROWS, D_MODEL), F32).at[:batch].set(c).at[ctx_row].set(c_ctx)
    mod = _adaln(cvec, w_mod, b_mod).reshape(depth, MOD_ROWS, N_MOD, D_MODEL)

    wgu = ffn_w_gu.astype(BF16)
    wd = ffn_w_down.astype(BF16)
    w_in_b = w_in.astype(BF16)
    w_out_b = w_out.astype(BF16)
    ws = gmlp_ws.astype(BF16).reshape(depth, N_GMLP_GROUPS * GMLP_CHUNK, GMLP_CHUNK)
    bs = jnp.repeat(jnp.swapaxes(gmlp_bs, 1, 2), HEAD_DIM, axis=2)
    cs = _channel_dft_matrix()
    dm_lat = _dft_matrix(seq)
    dm_ctx = _dft_matrix(ctx_len)
    pos_tab = _pos_table()

    h = x.reshape(batch * seq, D_MODEL)
    hc = ctx.reshape(batch * ctx_len, D_MODEL)
    lat = dict(tiles_per_seq=tiles_per_seq)
    cx = dict(fixed_row=ctx_row)

    for l in range(depth):
        last = l == depth - 1
        m = mod[l]
        wg, bg = _lru_gate_weights(lru_w_gates[l], lru_b_gates[l])

        h = _ffn(h, m, norm_g[l, 0], wgu[l, 0], wd[l, 0], k0=0,
                 pos_tab=pos_tab if l == 0 else None, batch=batch, **lat)
        hc = _ffn(hc, m, norm_g[l, 0], wgu[l, 0], wd[l, 0], k0=0, **cx)

        xa_c, ga_c, gu_c, gv_c, gf_c = _mix_in(hc, m, norm_g[l, 1], w_in_b[l], cs, **cx)
        zero_state = jnp.zeros((batch, 2, D_LRU), F32)
        ya_c, state_c = _lru(xa_c, ga_c, conv_w[l], conv_b[l], wg, bg, lru_lambda[l], zero_state,
                             batch=batch, seq=ctx_len)
        xa, ga, gu, gv, gf = _mix_in(h, m, norm_g[l, 1], w_in_b[l], cs, **lat)
        ya, _ = _lru(xa, ga, conv_w[l], conv_b[l], wg, bg, lru_lambda[l], state_c,
                     batch=batch, seq=seq)
        yc = _dft(dm_lat, gf, batch=batch, seq=seq).reshape(batch * seq, D_FNET)
        h = _mix_out(h, m, ya, gu, gv, yc, ws[l], bs[l], w_out_b[l], **lat)

        h = _ffn(h, m, norm_g[l, 2], wgu[l, 1], wd[l, 1], k0=6,
                 final_g=final_norm_g if last else None, **lat)
        if not last:
            yc_c = _dft(dm_ctx, gf_c, batch=batch, seq=ctx_len).reshape(batch * ctx_len, D_FNET)
            hc = _mix_out(hc, m, ya_c, gu_c, gv_c, yc_c, ws[l], bs[l], w_out_b[l], **cx)
            hc = _ffn(hc, m, norm_g[l, 2], wgu[l, 1], wd[l, 1], k0=6, **cx)
    return h.reshape(batch, seq, D_MODEL)
```

```python
import functools
import math

import jax
import jax.numpy as jnp
from jax import lax
from jax.experimental import pallas as pl
from jax.experimental.pallas import tpu as pltpu

D_MODEL = 1024
GRID_W = 64
HEAD_DIM = 64
D_FF = ((8 * D_MODEL // 3 + 127) // 128) * 128
D_LRU = D_MODEL // 2
N_LRU_HEADS = D_LRU // HEAD_DIM
D_GMLP = D_MODEL // 4
N_GMLP_GROUPS = D_GMLP // HEAD_DIM
GMLP_CHUNK = 128
D_FNET = D_MODEL // 4
N_FNET_GROUPS = D_FNET // HEAD_DIM
D_IN = 2 * D_LRU + 2 * D_GMLP + D_FNET
CONV_W = 4
LRU_C = 8.0
N_MOD = 9
EPS = 1e-6

LANES = 128
SUBLANES = 8
VMEM_LIMIT_BYTES = 56 * 1024 * 1024

ROW_TILE = 512
FF_CHUNK = 256
MOD_ROWS = 16
LRU_GROUP = LANES
N_LRU_GROUPS = D_LRU // LRU_GROUP
N_SEG = SUBLANES
LRU_CHUNK = 64
DFT_RADIX = 64

BF16 = jnp.bfloat16
F32 = jnp.float32
F32_TINY = float(jnp.finfo(jnp.float32).tiny)
LOG2_E = math.log2(math.e)


def _cparams(sem):
    return pltpu.CompilerParams(dimension_semantics=sem, vmem_limit_bytes=VMEM_LIMIT_BYTES)


def _resident(shape, index_map):
    return pl.BlockSpec(shape, index_map, pipeline_mode=pl.Buffered(1))


def _adaln_kernel(c_ref, w_ref, b_ref, o_ref):
    c = c_ref[...]
    s = (c * jax.nn.sigmoid(c)).astype(BF16)
    o_ref[...] = jnp.dot(s, w_ref[...].astype(BF16), preferred_element_type=F32) + b_ref[...]


def _adaln(cvec, w_mod, b_mod):
    depth = w_mod.shape[0]
    return pl.pallas_call(
        _adaln_kernel,
        grid=(depth, N_MOD),
        in_specs=[
            pl.BlockSpec((MOD_ROWS, D_MODEL), lambda l, j: (0, 0)),
            pl.BlockSpec((None, D_MODEL, D_MODEL), lambda l, j: (l, 0, j)),
            pl.BlockSpec((None, 1, D_MODEL), lambda l, j: (l, 0, j)),
        ],
        out_specs=pl.BlockSpec((None, MOD_ROWS, D_MODEL), lambda l, j: (l, 0, j)),
        out_shape=jax.ShapeDtypeStruct((depth, MOD_ROWS, N_MOD * D_MODEL), F32),
        compiler_params=_cparams(("arbitrary", "arbitrary")),
        name="adaln",
    )(cvec, w_mod, b_mod.reshape(depth, 1, N_MOD * D_MODEL))


def _modulated_norm(h, g, shift, scale):
    y = h * lax.rsqrt(jnp.mean(h * h, axis=-1, keepdims=True) + EPS)
    return ((y * g) * (1.0 + scale) + shift).astype(BF16)


def _mod_spec(layer, tiles_per_seq, fixed_row):
    if fixed_row is not None:
        index = lambda i: (layer, fixed_row, 0, 0)
    else:
        index = lambda i: (layer, i // tiles_per_seq, 0, 0)
    return pl.BlockSpec((None, None, N_MOD, D_MODEL), index)


def _row_spec(width):
    return pl.BlockSpec((ROW_TILE, width), lambda i: (i, 0))


def _layer_resident(shape, *lead):
    zeros = (0,) * len(shape)
    return _resident((None,) * len(lead) + shape, lambda i: lead + zeros)


def _mixer_output(h, gate, ya_ref, gu_ref, gv_ref, yc_ref, ws_ref, bs_ref, wo_ref):
    tm = h.shape[0]
    half = D_GMLP // 2
    lane = lax.broadcasted_iota(jnp.int32, (GMLP_CHUNK, half), 1)
    first_group = lane < HEAD_DIM
    yb = []
    for c in range(tm // GMLP_CHUNK):
        rows = slice(c * GMLP_CHUNK, (c + 1) * GMLP_CHUNK)
        cols = []
        for p in range(2):
            m = jnp.dot(ws_ref[p * 2 * GMLP_CHUNK:(p + 1) * 2 * GMLP_CHUNK, :],
                        gv_ref[rows, p * half:(p + 1) * half], preferred_element_type=F32)
            cols.append(jnp.where(first_group, m[:GMLP_CHUNK], m[GMLP_CHUNK:]))
        mixed = jnp.concatenate(cols, axis=1) + bs_ref[...]
        yb.append((gu_ref[rows, :].astype(F32) * mixed).astype(BF16))
    yb = jnp.concatenate(yb, axis=0)
    y = jnp.dot(ya_ref[...], wo_ref[0:D_LRU, :], preferred_element_type=F32)
    y = y + jnp.dot(yb, wo_ref[D_LRU:D_LRU + D_GMLP, :], preferred_element_type=F32)
    y = y + jnp.dot(yc_ref[...], wo_ref[D_LRU + D_GMLP:, :], preferred_element_type=F32)
    return h + gate * y


def _ffn_kernel(*refs, k0, embed, mix, final_norm):
    refs = list(refs)
    h_ref = refs.pop(0)
    tab_ref = refs.pop(0) if embed else None
    mod_ref = refs.pop(0)
    mix_refs = [refs.pop(0) for _ in range(7)] if mix else None
    g_ref, wgu_ref, wd_ref = refs.pop(0), refs.pop(0), refs.pop(0)
    gf_ref = refs.pop(0) if final_norm else None
    (o_ref,) = refs

    h = h_ref[...]
    if embed:
        tm = h.shape[0]
        n_grid_rows = tm // GRID_W
        tiles_per_seq = pl.num_programs(0) // embed
        r0 = (pl.program_id(0) % tiles_per_seq) * n_grid_rows
        tab = tab_ref[...]
        row_part = jnp.concatenate(
            [jnp.broadcast_to(tab_ref[pl.ds(r0 + a, 1), :], (GRID_W, D_MODEL // 2))
             for a in range(n_grid_rows)], axis=0)
        col_part = jnp.concatenate([tab] * n_grid_rows, axis=0)
        h = h + jnp.concatenate([row_part, col_part], axis=1)
    if mix:
        h = _mixer_output(h, mod_ref[5:6, :], *mix_refs)

    shift = mod_ref[k0:k0 + 1, :]
    scale = mod_ref[k0 + 1:k0 + 2, :]
    gate = mod_ref[k0 + 2:k0 + 3, :]
    n = _modulated_norm(h, g_ref[...], shift, scale)

    acc = jnp.zeros(h.shape, F32)
    for j in range(D_FF // FF_CHUNK):
        lo = j * FF_CHUNK
        gt = jnp.dot(n, wgu_ref[:, lo:lo + FF_CHUNK], preferred_element_type=F32)
        up = jnp.dot(n, wgu_ref[:, D_FF + lo:D_FF + lo + FF_CHUNK], preferred_element_type=F32)
        a = (gt * jax.nn.sigmoid(gt) * up).astype(BF16)
        acc = acc + jnp.dot(a, wd_ref[lo:lo + FF_CHUNK, :], preferred_element_type=F32)
    out = h + 0.5 * gate * acc
    if final_norm:
        out = out * lax.rsqrt(jnp.mean(out * out, axis=-1, keepdims=True) + EPS) * gf_ref[...]
    o_ref[...] = out


def _ffn(h, mod, norm_g, wgu, wd, *, layer, which, tiles_per_seq=None, fixed_row=None,
         pos_tab=None, batch=None, mix=None, final_g=None):
    rows = h.shape[0]
    in_specs = [_row_spec(D_MODEL)]
    args = [h]
    if pos_tab is not None:
        in_specs.append(_resident((GRID_W, D_MODEL // 2), lambda i: (0, 0)))
        args.append(pos_tab)
    in_specs.append(_mod_spec(layer, tiles_per_seq, fixed_row))
    args.append(mod)
    if mix is not None:
        ya, gu, gv, yc, ws, bs, w_out = mix
        in_specs += [
            _row_spec(D_LRU), _row_spec(D_GMLP), _row_spec(D_GMLP), _row_spec(D_FNET),
            _layer_resident((N_GMLP_GROUPS * GMLP_CHUNK, GMLP_CHUNK), layer),
            _layer_resident((GMLP_CHUNK, D_GMLP), layer),
            _layer_resident((D_MODEL, D_MODEL), layer),
        ]
        args += [ya, gu, gv, yc, ws, bs, w_out]
    in_specs += [
        _layer_resident((1, D_MODEL), layer, 2 * which),
        _layer_resident((D_MODEL, 2 * D_FF), layer, which),
        _layer_resident((D_FF, D_MODEL), layer, which),
    ]
    args += [norm_g, wgu, wd]
    if final_g is not None:
        in_specs.append(_resident((1, D_MODEL), lambda i: (0, 0)))
        args.append(final_g.reshape(1, D_MODEL))
    kern = functools.partial(_ffn_kernel, k0=6 * which, embed=batch if pos_tab is not None else 0,
                             mix=mix is not None, final_norm=final_g is not None)
    return pl.pallas_call(
        kern,
        grid=(rows // ROW_TILE,),
        in_specs=in_specs,
        out_specs=_row_spec(D_MODEL),
        out_shape=jax.ShapeDtypeStruct((rows, D_MODEL), F32),
        compiler_params=_cparams(("arbitrary",)),
        name="ffn",
    )(*args)


def _mix_in_kernel(h_ref, mod_ref, g_ref, w_ref, cs_ref, xa_ref, ga_ref, gu_ref, gv_ref, gf_ref):
    n = _modulated_norm(h_ref[...], g_ref[...], mod_ref[3:4, :], mod_ref[4:5, :])

    def proj(lo, width):
        return jnp.dot(n, w_ref[:, lo:lo + width], preferred_element_type=F32)

    xa_ref[...] = proj(0, D_LRU)
    ga_ref[...] = jax.nn.gelu(proj(D_LRU, D_LRU)).astype(BF16)
    gu_ref[...] = jax.nn.gelu(proj(2 * D_LRU, D_GMLP)).astype(BF16)
    gv_ref[...] = jax.nn.gelu(proj(2 * D_LRU + D_GMLP, D_GMLP)).astype(BF16)
    f = proj(2 * D_LRU + 2 * D_GMLP, D_FNET).astype(BF16)
    gf_ref[...] = jnp.dot(f, cs_ref[...], preferred_element_type=F32).astype(BF16)


def _mix_in(h, mod, norm_g, w_in, cs, *, layer, tiles_per_seq=None, fixed_row=None):
    rows = h.shape[0]
    return pl.pallas_call(
        _mix_in_kernel,
        grid=(rows // ROW_TILE,),
        in_specs=[
            _row_spec(D_MODEL),
            _mod_spec(layer, tiles_per_seq, fixed_row),
            _layer_resident((1, D_MODEL), layer, 1),
            _layer_resident((D_MODEL, D_IN), layer),
            _resident((D_FNET, 2 * D_FNET), lambda i: (0, 0)),
        ],
        out_specs=[_row_spec(D_LRU), _row_spec(D_LRU), _row_spec(D_GMLP), _row_spec(D_GMLP),
                   _row_spec(2 * D_FNET)],
        out_shape=[
            jax.ShapeDtypeStruct((rows, D_LRU), F32),
            jax.ShapeDtypeStruct((rows, D_LRU), BF16),
            jax.ShapeDtypeStruct((rows, D_GMLP), BF16),
            jax.ShapeDtypeStruct((rows, D_GMLP), BF16),
            jax.ShapeDtypeStruct((rows, 2 * D_FNET), BF16),
        ],
        compiler_params=_cparams(("arbitrary",)),
        name="mix_in",
    )(h, mod, norm_g, w_in, cs)


def _lru_coefficients(g, xh, c_half, n_c_half_log2e):
    t_r = jnp.tanh(g[:, :LRU_GROUP])
    t_i = jnp.tanh(g[:, LRU_GROUP:])
    m = c_half * t_r + c_half
    a = jnp.exp2(n_c_half_log2e * t_r + n_c_half_log2e)
    z = jnp.tanh(m) * (a * a + 1.0)
    root = z * lax.rsqrt(jnp.maximum(z, F32_TINY))
    b = (root * xh) * (t_i + 1.0)
    return a, b


def _lru_kernel(xa_ref, ga_ref, cw_ref, cb_ref, wg_ref, bg_ref, lam_ref, h0_ref,
                ya_ref, st_ref, xin_ref, xc_ref, pf_ref, hf_ref, pb_ref, hb_ref, *, seq):
    seg = seq // N_SEG
    chunk = min(LRU_CHUNK, seg)
    blk = chunk * N_SEG
    n_blk = seg // chunk
    v = N_SEG

    for j in range(N_SEG):
        xin_ref[pl.ds(2 * v + j, seg, stride=N_SEG), :] = xa_ref[j * seg:(j + 1) * seg, :]
    zrow = jnp.zeros((1, LRU_GROUP), F32)
    for k in (0, 1):
        tail = xin_ref[(seg + k) * v:(seg + k + 1) * v, :]
        xin_ref[k * v:(k + 1) * v, :] = jnp.concatenate([zrow, tail[:N_SEG - 1]], axis=0)
    head = xin_ref[2 * v:3 * v, :]
    xin_ref[(seg + 2) * v:(seg + 3) * v, :] = jnp.concatenate([head[1:], zrow], axis=0)

    cw = cw_ref[...]
    cb = cb_ref[...]
    for c in range(n_blk):
        acc = cb
        for k in range(CONV_W):
            acc = acc + xin_ref[c * blk + k * v:c * blk + k * v + blk, :] * cw[k:k + 1, :]
        xc_ref[c * blk:(c + 1) * blk, :] = acc

    c_half = (0.5 * LRU_C) * jax.nn.softplus(-lam_ref[...])
    n_c_half_log2e = -LOG2_E * c_half
    bg = bg_ref[...]

    def block(c, d, state, p_ref, h_ref):
        rows = slice(c * blk, (c + 1) * blk)
        xc = xc_ref[rows, :]
        cols = slice(2 * d * LRU_GROUP, (2 * d + 2) * LRU_GROUP)
        g = jnp.dot(xc.astype(BF16), wg_ref[:, cols], preferred_element_type=F32) + bg[:, cols]
        a, b = _lru_coefficients(g, 0.5 * xc, c_half[d:d + 1, :], n_c_half_log2e[d:d + 1, :])
        p, h = state
        ps, hs = [], []
        steps = range(chunk) if d == 0 else reversed(range(chunk))
        for s in steps:
            a_s = a[s * v:(s + 1) * v, :]
            p = a_s * p
            h = a_s * h + b[s * v:(s + 1) * v, :]
            ps.append(p)
            hs.append(h)
        if d == 1:
            ps, hs = ps[::-1], hs[::-1]
        p_ref[rows, :] = jnp.concatenate(ps, axis=0)
        h_ref[rows, :] = jnp.concatenate(hs, axis=0)
        return p, h

    one = jnp.ones((N_SEG, LRU_GROUP), F32)
    zero = jnp.zeros((N_SEG, LRU_GROUP), F32)
    fwd = bwd = (one, zero)
    for c in range(n_blk):
        fwd = block(c, 0, fwd, pf_ref, hf_ref)
        bwd = block(n_blk - 1 - c, 1, bwd, pb_ref, hb_ref)
    (pf, hf), (pb, hb) = fwd, bwd

    h0 = h0_ref[...]
    c = h0[0:1, :]
    rows = []
    for j in range(N_SEG):
        rows.append(c)
        c = hf[j:j + 1, :] + pf[j:j + 1, :] * c
    carry_f = jnp.concatenate(rows * chunk, axis=0)
    final_f = c
    c = h0[1:2, :]
    rows = []
    for j in reversed(range(N_SEG)):
        rows.append(c)
        c = hb[j:j + 1, :] + pb[j:j + 1, :] * c
    carry_b = jnp.concatenate(rows[::-1] * chunk, axis=0)
    final_b = c
    st_ref[...] = jnp.concatenate([final_f, final_b], axis=0)

    for c in range(n_blk):
        rows = slice(c * blk, (c + 1) * blk)
        xin_ref[rows, :] = ((hf_ref[rows, :] + pf_ref[rows, :] * carry_f)
                            + (hb_ref[rows, :] + pb_ref[rows, :] * carry_b))
    for j in range(N_SEG):
        y = xin_ref[pl.ds(j, seg, stride=N_SEG), :]
        ga = ga_ref[j * seg:(j + 1) * seg, :].astype(F32)
        ya_ref[j * seg:(j + 1) * seg, :] = (y * ga).astype(BF16)


def _lru(xa, ga, conv_w, conv_b, wg, bg, lam, h0, *, layer, batch, seq):
    blk = lambda: pl.BlockSpec((seq, LRU_GROUP), lambda b, g: (b, g))
    grp = lambda r: pl.BlockSpec((None, r, LRU_GROUP), lambda b, g: (layer, 0, g))
    st = lambda: pl.BlockSpec((None, 2, LRU_GROUP), lambda b, g: (b, 0, g))
    seg = seq // N_SEG
    return pl.pallas_call(
        functools.partial(_lru_kernel, seq=seq),
        grid=(batch, N_LRU_GROUPS),
        in_specs=[
            blk(), blk(), grp(CONV_W), grp(1),
            pl.BlockSpec((None, None, LRU_GROUP, 4 * LRU_GROUP), lambda b, g: (layer, g, 0, 0)),
            pl.BlockSpec((None, None, 1, 4 * LRU_GROUP), lambda b, g: (layer, g, 0, 0)),
            grp(2), st(),
        ],
        out_specs=[blk(), st()],
        out_shape=[
            jax.ShapeDtypeStruct((batch * seq, D_LRU), BF16),
            jax.ShapeDtypeStruct((batch, 2, D_LRU), F32),
        ],
        scratch_shapes=[pltpu.VMEM(((seg + CONV_W - 1) * N_SEG, LRU_GROUP), F32)]
        + [pltpu.VMEM((seq, LRU_GROUP), F32) for _ in range(5)],
        compiler_params=_cparams(("arbitrary", "arbitrary")),
        name="lru",
    )(xa, ga, conv_w, conv_b, wg, bg, lam, h0)


def _lru_gate_weights(w_gates, b_gates):
    depth = w_gates.shape[0]
    heads_per_group = LRU_GROUP // HEAD_DIM
    w = 0.5 * w_gates.reshape(depth, 2, 2, N_LRU_GROUPS, heads_per_group, HEAD_DIM, HEAD_DIM)
    eye = jnp.eye(heads_per_group, dtype=w.dtype)
    bd = w[:, :, :, :, :, :, None, :] * eye[None, None, None, None, :, None, :, None]
    bd = bd.reshape(depth, 2, 2, N_LRU_GROUPS, LRU_GROUP, LRU_GROUP)
    wg = jnp.transpose(bd, (0, 3, 4, 1, 2, 5)).reshape(depth, N_LRU_GROUPS, LRU_GROUP, 4 * LRU_GROUP)
    b = 0.5 * b_gates.reshape(depth, 2, 2, N_LRU_GROUPS, LRU_GROUP)
    bg = jnp.transpose(b, (0, 3, 1, 2, 4)).reshape(depth, N_LRU_GROUPS, 1, 4 * LRU_GROUP)
    return wg.astype(BF16), bg


def _dft_kernel(dm_ref, g_ref, o_ref, acc_ref):
    kk = pl.program_id(1)

    @pl.when(kk == 0)
    def _():
        acc_ref[...] = jnp.zeros_like(acc_ref)

    dm = dm_ref[...]
    for b in range(g_ref.shape[0]):
        acc_ref[b] += jnp.dot(dm, g_ref[b], preferred_element_type=F32)

    @pl.when(kk == pl.num_programs(1) - 1)
    def _():
        o_ref[...] = acc_ref[...].astype(BF16)


def _dft(dm, gf, *, batch, seq):
    tm = min(seq, 1024)
    tk = min(seq, 1024)
    nk = seq // tk
    return pl.pallas_call(
        _dft_kernel,
        grid=(seq // tm, 2 * nk),
        in_specs=[
            pl.BlockSpec((tm, tk), lambda i, k: (i, k)),
            pl.BlockSpec((batch, tk, D_FNET), lambda i, k: (0, k % nk, k // nk)),
        ],
        out_specs=pl.BlockSpec((batch, tm, D_FNET), lambda i, k: (0, i, 0)),
        out_shape=jax.ShapeDtypeStruct((batch, seq, D_FNET), BF16),
        scratch_shapes=[pltpu.VMEM((batch, tm, D_FNET), F32)],
        compiler_params=_cparams(("arbitrary", "arbitrary")),
        name="dft",
    )(dm, gf.reshape(batch, seq, 2 * D_FNET))


def _dft_matrix(seq):
    scale = 1.0 / math.sqrt(seq * HEAD_DIM)
    r = min(DFT_RADIX, seq)
    n = jnp.arange(seq, dtype=jnp.int32)[None, :]
    j = jnp.arange(r, dtype=jnp.int32)[:, None]
    hi = jnp.arange(seq // r, dtype=jnp.int32)[:, None]
    ang_lo = ((j * n) % seq).astype(F32) * (2.0 * math.pi / seq)
    ang_hi = ((hi * r * n) % seq).astype(F32) * (2.0 * math.pi / seq)
    cl, sl = jnp.cos(ang_lo)[None], jnp.sin(ang_lo)[None]
    ch, sh = (jnp.cos(ang_hi) * scale)[:, None], (jnp.sin(ang_hi) * scale)[:, None]
    cos = (ch * cl - sh * sl).reshape(seq, seq)
    msin = -(sh * cl + ch * sl).reshape(seq, seq)
    return jnp.concatenate([cos, msin], axis=1).astype(BF16)


def _channel_dft_matrix():
    c = jnp.arange(D_FNET, dtype=jnp.int32)
    same_group = (c[:, None] // HEAD_DIM) == (c[None, :] // HEAD_DIM)
    ang = (((c[:, None] % HEAD_DIM) * (c[None, :] % HEAD_DIM)) % HEAD_DIM).astype(F32) * (
        2.0 * math.pi / HEAD_DIM)
    cs = jnp.concatenate([jnp.where(same_group, jnp.cos(ang), 0.0),
                          jnp.where(same_group, jnp.sin(ang), 0.0)], axis=1)
    return cs.astype(BF16)


def _pos_table_kernel(f_ref, o_ref):
    j = lax.broadcasted_iota(jnp.int32, (GRID_W, D_MODEL // 4), 0).astype(F32)
    e = j * f_ref[...]
    o_ref[...] = jnp.concatenate([jnp.sin(e), jnp.cos(e)], axis=1)


def _pos_table():
    q = D_MODEL // 4
    freqs = 1.0 / (10000.0 ** (jnp.arange(q, dtype=F32) / q))
    return pl.pallas_call(
        _pos_table_kernel,
        out_shape=jax.ShapeDtypeStruct((GRID_W, D_MODEL // 2), F32),
        name="pos_table",
    )(freqs.reshape(1, q))


def kernel(x, c, ctx, c_ctx, w_mod, b_mod, norm_g, ffn_w_gu, ffn_w_down, w_in, w_out, conv_w, conv_b,
           lru_w_gates, lru_b_gates, lru_lambda, gmlp_ws, gmlp_bs, final_norm_g):
    batch, seq, _ = x.shape
    ctx_len = ctx.shape[1]
    depth = w_mod.shape[0]
    assert batch < MOD_ROWS and seq % ROW_TILE == 0 and (batch * ctx_len) % ROW_TILE == 0
    assert seq % (N_SEG * LRU_CHUNK) == 0 and ctx_len % (N_SEG * SUBLANES) == 0
    assert LRU_CHUNK % (ctx_len // N_SEG) == 0 or (ctx_len // N_SEG) % LRU_CHUNK == 0
    ctx_row = batch
    lat = dict(tiles_per_seq=seq // ROW_TILE)
    cx = dict(fixed_row=ctx_row)

    cvec = jnp.zeros((MOD_ROWS, D_MODEL), F32).at[:batch].set(c).at[ctx_row].set(c_ctx)
    mod = _adaln(cvec, w_mod, b_mod).reshape(depth, MOD_ROWS, N_MOD, D_MODEL)

    wgu = ffn_w_gu.astype(BF16)
    wd = ffn_w_down.astype(BF16)
    w_in_b = w_in.astype(BF16)
    w_out_b = w_out.astype(BF16)
    norm_g4 = norm_g.reshape(depth, 3, 1, D_MODEL)
    ws = gmlp_ws.astype(BF16).reshape(depth, N_GMLP_GROUPS * GMLP_CHUNK, GMLP_CHUNK)
    bs = jnp.repeat(jnp.swapaxes(gmlp_bs, 1, 2), HEAD_DIM, axis=2)
    wg, bg = _lru_gate_weights(lru_w_gates, lru_b_gates)
    conv_b3 = conv_b.reshape(depth, 1, D_LRU)
    cs = _channel_dft_matrix()
    dm_lat = _dft_matrix(seq)
    dm_ctx = _dft_matrix(ctx_len)
    pos_tab = _pos_table()

    h = x.reshape(batch * seq, D_MODEL)
    hc = ctx.reshape(batch * ctx_len, D_MODEL)
    zero_state = jnp.zeros((batch, 2, D_LRU), F32)
    mix = mix_c = None

    for l in range(depth):
        last = l == depth - 1
        ffn = functools.partial(_ffn, mod=mod, norm_g=norm_g4, wgu=wgu, wd=wd, layer=l)
        mix_in = functools.partial(_mix_in, mod=mod, norm_g=norm_g4, w_in=w_in_b, cs=cs, layer=l)
        lru = functools.partial(_lru, conv_w=conv_w, conv_b=conv_b3, wg=wg, bg=bg, lam=lru_lambda,
                                layer=l, batch=batch)

        h = ffn(h, which=0, pos_tab=pos_tab if l == 0 else None, batch=batch, **lat)
        hc = ffn(hc, which=0, **cx)

        xa_c, ga_c, gu_c, gv_c, gf_c = mix_in(hc, **cx)
        ya_c, state_c = lru(xa_c, ga_c, h0=zero_state, seq=ctx_len)
        xa, ga, gu, gv, gf = mix_in(h, **lat)
        ya, _ = lru(xa, ga, h0=state_c, seq=seq)
        yc = _dft(dm_lat, gf, batch=batch, seq=seq).reshape(batch * seq, D_FNET)

        h = ffn(h, which=1, mix=(ya, gu, gv, yc, ws, bs, w_out_b),
                final_g=final_norm_g if last else None, **lat)
        if not last:
            yc_c = _dft(dm_ctx, gf_c, batch=batch, seq=ctx_len).reshape(batch * ctx_len, D_FNET)
            hc = ffn(hc, which=1, mix=(ya_c, gu_c, gv_c, yc_c, ws, bs, w_out_b), **cx)
    return h.reshape(batch, seq, D_MODEL)
```

```python
import functools
import math

import jax
import jax.numpy as jnp
from jax import lax
from jax.experimental import pallas as pl
from jax.experimental.pallas import tpu as pltpu

D_MODEL = 1024
GRID_W = 64
HEAD_DIM = 64
D_FF = ((8 * D_MODEL // 3 + 127) // 128) * 128
D_LRU = D_MODEL // 2
N_LRU_HEADS = D_LRU // HEAD_DIM
D_GMLP = D_MODEL // 4
N_GMLP_GROUPS = D_GMLP // HEAD_DIM
GMLP_CHUNK = 128
D_FNET = D_MODEL // 4
N_FNET_GROUPS = D_FNET // HEAD_DIM
D_IN = 2 * D_LRU + 2 * D_GMLP + D_FNET
CONV_W = 4
LRU_C = 8.0
N_MOD = 9
EPS = 1e-6

LANES = 128
SUBLANES = 8
VMEM_LIMIT_BYTES = 56 * 1024 * 1024

ROW_TILE = 512
FF_CHUNK = 256
MOD_ROWS = 16
LRU_GROUP = LANES
N_LRU_GROUPS = D_LRU // LRU_GROUP
N_SEG = SUBLANES
LRU_CHUNK = 64
DFT_Q = 256

BF16 = jnp.bfloat16
F32 = jnp.float32
F32_TINY = float(jnp.finfo(jnp.float32).tiny)
LOG2_E = math.log2(math.e)


def _cparams(sem):
    return pltpu.CompilerParams(dimension_semantics=sem, vmem_limit_bytes=VMEM_LIMIT_BYTES)


def _resident(shape, index_map):
    return pl.BlockSpec(shape, index_map, pipeline_mode=pl.Buffered(1))


def _adaln_kernel(c_ref, w_ref, b_ref, o_ref):
    c = c_ref[...]
    s = (c * jax.nn.sigmoid(c)).astype(BF16)
    o_ref[...] = jnp.dot(s, w_ref[...].astype(BF16), preferred_element_type=F32) + b_ref[...]


def _adaln(cvec, w_mod, b_mod):
    depth = w_mod.shape[0]
    return pl.pallas_call(
        _adaln_kernel,
        grid=(depth, N_MOD),
        in_specs=[
            pl.BlockSpec((MOD_ROWS, D_MODEL), lambda l, j: (0, 0)),
            pl.BlockSpec((None, D_MODEL, D_MODEL), lambda l, j: (l, 0, j)),
            pl.BlockSpec((None, 1, D_MODEL), lambda l, j: (l, 0, j)),
        ],
        out_specs=pl.BlockSpec((None, MOD_ROWS, D_MODEL), lambda l, j: (l, 0, j)),
        out_shape=jax.ShapeDtypeStruct((depth, MOD_ROWS, N_MOD * D_MODEL), F32),
        compiler_params=_cparams(("arbitrary", "arbitrary")),
        name="adaln",
    )(cvec, w_mod, b_mod.reshape(depth, 1, N_MOD * D_MODEL))


def _modulated_norm(h, g, shift, scale):
    y = h * lax.rsqrt(jnp.mean(h * h, axis=-1, keepdims=True) + EPS)
    return ((y * g) * (1.0 + scale) + shift).astype(BF16)


def _mod_spec(layer, tiles_per_seq, fixed_row):
    if fixed_row is not None:
        index = lambda i: (layer, fixed_row, 0, 0)
    else:
        index = lambda i: (layer, i // tiles_per_seq, 0, 0)
    return pl.BlockSpec((None, None, N_MOD, D_MODEL), index)


def _row_spec(width):
    return pl.BlockSpec((ROW_TILE, width), lambda i: (i, 0))


def _layer_resident(shape, *lead):
    zeros = (0,) * len(shape)
    return _resident((None,) * len(lead) + shape, lambda i: lead + zeros)


def _mixer_output(h, gate, ya_ref, gu_ref, gv_ref, yc_ref, ws_ref, bs_ref, wo_ref):
    tm = h.shape[0]
    half = D_GMLP // 2
    lane = lax.broadcasted_iota(jnp.int32, (GMLP_CHUNK, half), 1)
    first_group = lane < HEAD_DIM
    yb = []
    for c in range(tm // GMLP_CHUNK):
        rows = slice(c * GMLP_CHUNK, (c + 1) * GMLP_CHUNK)
        cols = []
        for p in range(2):
            m = jnp.dot(ws_ref[p * 2 * GMLP_CHUNK:(p + 1) * 2 * GMLP_CHUNK, :],
                        gv_ref[rows, p * half:(p + 1) * half], preferred_element_type=F32)
            cols.append(jnp.where(first_group, m[:GMLP_CHUNK], m[GMLP_CHUNK:]))
        mixed = jnp.concatenate(cols, axis=1) + bs_ref[...]
        yb.append((gu_ref[rows, :].astype(F32) * mixed).astype(BF16))
    yb = jnp.concatenate(yb, axis=0)
    y = jnp.dot(ya_ref[...], wo_ref[0:D_LRU, :], preferred_element_type=F32)
    y = y + jnp.dot(yb, wo_ref[D_LRU:D_LRU + D_GMLP, :], preferred_element_type=F32)
    y = y + jnp.dot(yc_ref[...], wo_ref[D_LRU + D_GMLP:, :], preferred_element_type=F32)
    return h + gate * y


def _mixer_input(h, mod_ref, g_ref, w_ref, xa_ref, ga_ref, gu_ref, gv_ref, f_ref, fs_ref, radix):
    n = _modulated_norm(h, g_ref[...], mod_ref[3:4, :], mod_ref[4:5, :])

    def proj(lo, width):
        return jnp.dot(n, w_ref[:, lo:lo + width], preferred_element_type=F32)

    xa_ref[...] = proj(0, D_LRU)
    ga_ref[...] = jax.nn.gelu(proj(D_LRU, D_LRU)).astype(BF16)
    gu_ref[...] = jax.nn.gelu(proj(2 * D_LRU, D_GMLP)).astype(BF16)
    gv_ref[...] = jax.nn.gelu(proj(2 * D_LRU + D_GMLP, D_GMLP)).astype(BF16)
    f = proj(2 * D_LRU + 2 * D_GMLP, D_FNET)
    if radix == 1:
        f_ref[...] = f.astype(BF16)
        return
    n_slabs = D_FNET // LANES
    for s in range(n_slabs):
        fs_ref[s] = f[:, s * LANES:(s + 1) * LANES]
    per_class = h.shape[0] // radix
    for a in range(radix):
        piece = [fs_ref[s, pl.ds(a, per_class, stride=radix), :] for s in range(n_slabs)]
        f_ref[a] = jnp.concatenate(piece, axis=1).astype(BF16)


def _ffn_kernel(*refs, k0, embed, mix, proj_radix, final_norm):
    refs = list(refs)
    take = lambda n: [refs.pop(0) for _ in range(n)]
    (h_ref,) = take(1)
    (tab_ref,) = take(1) if embed else (None,)
    (mod_ref,) = take(1)
    mix_refs = take(7) if mix else None
    g_ref, wgu_ref, wd_ref = take(3)
    proj_in = take(2) if proj_radix else None
    (gf_ref,) = take(1) if final_norm else (None,)
    (o_ref,) = take(1)
    proj_out = take(5) if proj_radix else None
    proj_scratch = take(1) if proj_radix and proj_radix > 1 else [None]
    assert not refs

    h = h_ref[...]
    if embed:
        tm = h.shape[0]
        n_grid_rows = tm // GRID_W
        tiles_per_seq = pl.num_programs(0) // embed
        r0 = (pl.program_id(0) % tiles_per_seq) * n_grid_rows
        tab = tab_ref[...]
        row_part = jnp.concatenate(
            [jnp.broadcast_to(tab_ref[pl.ds(r0 + a, 1), :], (GRID_W, D_MODEL // 2))
             for a in range(n_grid_rows)], axis=0)
        col_part = jnp.concatenate([tab] * n_grid_rows, axis=0)
        h = h + jnp.concatenate([row_part, col_part], axis=1)
    if mix:
        h = _mixer_output(h, mod_ref[5:6, :], *mix_refs)

    shift = mod_ref[k0:k0 + 1, :]
    scale = mod_ref[k0 + 1:k0 + 2, :]
    gate = mod_ref[k0 + 2:k0 + 3, :]
    n = _modulated_norm(h, g_ref[...], shift, scale)

    acc = jnp.zeros(h.shape, F32)
    for j in range(D_FF // FF_CHUNK):
        lo = j * FF_CHUNK
        gt = jnp.dot(n, wgu_ref[:, lo:lo + FF_CHUNK], preferred_element_type=F32)
        up = jnp.dot(n, wgu_ref[:, D_FF + lo:D_FF + lo + FF_CHUNK], preferred_element_type=F32)
        a = (gt * jax.nn.sigmoid(gt) * up).astype(BF16)
        acc = acc + jnp.dot(a, wd_ref[lo:lo + FF_CHUNK, :], preferred_element_type=F32)
    out = h + 0.5 * gate * acc
    if proj_radix:
        _mixer_input(out, mod_ref, *proj_in, *proj_out, *proj_scratch, proj_radix)
    if final_norm:
        out = out * lax.rsqrt(jnp.mean(out * out, axis=-1, keepdims=True) + EPS) * gf_ref[...]
    o_ref[...] = out


def _ffn(h, mod, norm_g, wgu, wd, *, layer, which, tiles_per_seq=None, fixed_row=None,
         pos_tab=None, batch=None, mix=None, proj=None, final_g=None):
    rows = h.shape[0]
    in_specs = [_row_spec(D_MODEL)]
    args = [h]
    if pos_tab is not None:
        in_specs.append(_resident((GRID_W, D_MODEL // 2), lambda i: (0, 0)))
        args.append(pos_tab)
    in_specs.append(_mod_spec(layer, tiles_per_seq, fixed_row))
    args.append(mod)
    if mix is not None:
        ya, gu, gv, yc, ws, bs, w_out = mix
        in_specs += [
            _row_spec(D_LRU), _row_spec(D_GMLP), _row_spec(D_GMLP), _row_spec(D_FNET),
            _layer_resident((N_GMLP_GROUPS * GMLP_CHUNK, GMLP_CHUNK), layer),
            _layer_resident((GMLP_CHUNK, D_GMLP), layer),
            _layer_resident((D_MODEL, D_MODEL), layer),
        ]
        args += [ya, gu, gv, yc, ws, bs, w_out]
    in_specs += [
        _layer_resident((1, D_MODEL), layer, 2 * which),
        _layer_resident((D_MODEL, 2 * D_FF), layer, which),
        _layer_resident((D_FF, D_MODEL), layer, which),
    ]
    args += [norm_g, wgu, wd]
    out_specs = [_row_spec(D_MODEL)]
    out_shape = [jax.ShapeDtypeStruct((rows, D_MODEL), F32)]
    scratch = []
    radix = 0
    if proj is not None:
        w_in, radix = proj
        in_specs += [_layer_resident((1, D_MODEL), layer, 1), _layer_resident((D_MODEL, D_IN), layer)]
        args += [norm_g, w_in]
        out_specs += [_row_spec(D_LRU), _row_spec(D_LRU), _row_spec(D_GMLP), _row_spec(D_GMLP)]
        out_shape += [
            jax.ShapeDtypeStruct((rows, D_LRU), F32),
            jax.ShapeDtypeStruct((rows, D_LRU), BF16),
            jax.ShapeDtypeStruct((rows, D_GMLP), BF16),
            jax.ShapeDtypeStruct((rows, D_GMLP), BF16),
        ]
        if radix == 1:
            out_specs.append(_row_spec(D_FNET))
            out_shape.append(jax.ShapeDtypeStruct((rows, D_FNET), BF16))
        else:
            per_class = ROW_TILE // radix
            out_specs.append(pl.BlockSpec(
                (None, radix, per_class, D_FNET),
                lambda i: (i // tiles_per_seq, 0, i % tiles_per_seq, 0)))
            out_shape.append(jax.ShapeDtypeStruct(
                (rows // (tiles_per_seq * ROW_TILE), radix, tiles_per_seq * per_class, D_FNET), BF16))
            scratch.append(pltpu.VMEM((D_FNET // LANES, ROW_TILE, LANES), F32))
    if final_g is not None:
        in_specs.append(_resident((1, D_MODEL), lambda i: (0, 0)))
        args.append(final_g.reshape(1, D_MODEL))
    kern = functools.partial(_ffn_kernel, k0=6 * which, embed=batch if pos_tab is not None else 0,
                             mix=mix is not None, proj_radix=radix, final_norm=final_g is not None)
    out = pl.pallas_call(
        kern,
        grid=(rows // ROW_TILE,),
        in_specs=in_specs,
        out_specs=out_specs,
        out_shape=out_shape,
        scratch_shapes=scratch,
        compiler_params=_cparams(("arbitrary",)),
        name="ffn",
    )(*args)
    return out if proj is not None else out[0]


def _lru_coefficients(g, xh, c_half, n_c_half_log2e):
    t_r = jnp.tanh(g[:, :LRU_GROUP])
    t_i = jnp.tanh(g[:, LRU_GROUP:])
    m = c_half * t_r + c_half
    a = jnp.exp2(n_c_half_log2e * t_r + n_c_half_log2e)
    z = jnp.tanh(m) * (a * a + 1.0)
    root = z * lax.rsqrt(jnp.maximum(z, F32_TINY))
    b = (root * xh) * (t_i + 1.0)
    return a, b


def _lru_kernel(xa_ref, ga_ref, cw_ref, cb_ref, wg_ref, bg_ref, lam_ref, h0_ref,
                ya_ref, st_ref, xin_ref, xc_ref, pf_ref, hf_ref, pb_ref, hb_ref, *, seq):
    seg = seq // N_SEG
    chunk = min(LRU_CHUNK, seg)
    blk = chunk * N_SEG
    n_blk = seg // chunk
    v = N_SEG

    for j in range(N_SEG):
        xin_ref[pl.ds(2 * v + j, seg, stride=N_SEG), :] = xa_ref[j * seg:(j + 1) * seg, :]
    zrow = jnp.zeros((1, LRU_GROUP), F32)
    for k in (0, 1):
        tail = xin_ref[(seg + k) * v:(seg + k + 1) * v, :]
        xin_ref[k * v:(k + 1) * v, :] = jnp.concatenate([zrow, tail[:N_SEG - 1]], axis=0)
    head = xin_ref[2 * v:3 * v, :]
    xin_ref[(seg + 2) * v:(seg + 3) * v, :] = jnp.concatenate([head[1:], zrow], axis=0)

    cw = cw_ref[...]
    cb = cb_ref[...]
    for c in range(n_blk):
        acc = cb
        for k in range(CONV_W):
            acc = acc + xin_ref[c * blk + k * v:c * blk + k * v + blk, :] * cw[k:k + 1, :]
        xc_ref[c * blk:(c + 1) * blk, :] = acc

    c_half = (0.5 * LRU_C) * jax.nn.softplus(-lam_ref[...])
    n_c_half_log2e = -LOG2_E * c_half
    bg = bg_ref[...]

    def block(c, d, state, p_ref, h_ref):
        rows = slice(c * blk, (c + 1) * blk)
        xc = xc_ref[rows, :]
        cols = slice(2 * d * LRU_GROUP, (2 * d + 2) * LRU_GROUP)
        g = jnp.dot(xc.astype(BF16), wg_ref[:, cols], preferred_element_type=F32) + bg[:, cols]
        a, b = _lru_coefficients(g, 0.5 * xc, c_half[d:d + 1, :], n_c_half_log2e[d:d + 1, :])
        p, h = state
        ps, hs = [], []
        steps = range(chunk) if d == 0 else reversed(range(chunk))
        for s in steps:
            a_s = a[s * v:(s + 1) * v, :]
            p = a_s * p
            h = a_s * h + b[s * v:(s + 1) * v, :]
            ps.append(p)
            hs.append(h)
        if d == 1:
            ps, hs = ps[::-1], hs[::-1]
        p_ref[rows, :] = jnp.concatenate(ps, axis=0)
        h_ref[rows, :] = jnp.concatenate(hs, axis=0)
        return p, h

    one = jnp.ones((N_SEG, LRU_GROUP), F32)
    zero = jnp.zeros((N_SEG, LRU_GROUP), F32)
    fwd = bwd = (one, zero)
    for c in range(n_blk):
        fwd = block(c, 0, fwd, pf_ref, hf_ref)
        bwd = block(n_blk - 1 - c, 1, bwd, pb_ref, hb_ref)
    (pf, hf), (pb, hb) = fwd, bwd

    h0 = h0_ref[...]
    c = h0[0:1, :]
    rows = []
    for j in range(N_SEG):
        rows.append(c)
        c = hf[j:j + 1, :] + pf[j:j + 1, :] * c
    carry_f = jnp.concatenate(rows * chunk, axis=0)
    final_f = c
    c = h0[1:2, :]
    rows = []
    for j in reversed(range(N_SEG)):
        rows.append(c)
        c = hb[j:j + 1, :] + pb[j:j + 1, :] * c
    carry_b = jnp.concatenate(rows[::-1] * chunk, axis=0)
    final_b = c
    st_ref[...] = jnp.concatenate([final_f, final_b], axis=0)

    for c in range(n_blk):
        rows = slice(c * blk, (c + 1) * blk)
        xin_ref[rows, :] = ((hf_ref[rows, :] + pf_ref[rows, :] * carry_f)
                            + (hb_ref[rows, :] + pb_ref[rows, :] * carry_b))
    for j in range(N_SEG):
        y = xin_ref[pl.ds(j, seg, stride=N_SEG), :]
        ga = ga_ref[j * seg:(j + 1) * seg, :].astype(F32)
        ya_ref[j * seg:(j + 1) * seg, :] = (y * ga).astype(BF16)


def _lru(xa, ga, conv_w, conv_b, wg, bg, lam, h0, *, layer, batch, seq):
    blk = lambda: pl.BlockSpec((seq, LRU_GROUP), lambda b, g: (b, g))
    grp = lambda r: pl.BlockSpec((None, r, LRU_GROUP), lambda b, g: (layer, 0, g))
    st = lambda: pl.BlockSpec((None, 2, LRU_GROUP), lambda b, g: (b, 0, g))
    seg = seq // N_SEG
    return pl.pallas_call(
        functools.partial(_lru_kernel, seq=seq),
        grid=(batch, N_LRU_GROUPS),
        in_specs=[
            blk(), blk(), grp(CONV_W), grp(1),
            pl.BlockSpec((None, None, LRU_GROUP, 4 * LRU_GROUP), lambda b, g: (layer, g, 0, 0)),
            pl.BlockSpec((None, None, 1, 4 * LRU_GROUP), lambda b, g: (layer, g, 0, 0)),
            grp(2), st(),
        ],
        out_specs=[blk(), st()],
        out_shape=[
            jax.ShapeDtypeStruct((batch * seq, D_LRU), BF16),
            jax.ShapeDtypeStruct((batch, 2, D_LRU), F32),
        ],
        scratch_shapes=[pltpu.VMEM(((seg + CONV_W - 1) * N_SEG, LRU_GROUP), F32)]
        + [pltpu.VMEM((seq, LRU_GROUP), F32) for _ in range(5)],
        compiler_params=_cparams(("arbitrary", "arbitrary")),
        name="lru",
    )(xa, ga, conv_w, conv_b, wg, bg, lam, h0)


def _lru_gate_weights(w_gates, b_gates):
    depth = w_gates.shape[0]
    heads_per_group = LRU_GROUP // HEAD_DIM
    w = 0.5 * w_gates.reshape(depth, 2, 2, N_LRU_GROUPS, heads_per_group, HEAD_DIM, HEAD_DIM)
    eye = jnp.eye(heads_per_group, dtype=w.dtype)
    bd = w[:, :, :, :, :, :, None, :] * eye[None, None, None, None, :, None, :, None]
    bd = bd.reshape(depth, 2, 2, N_LRU_GROUPS, LRU_GROUP, LRU_GROUP)
    wg = jnp.transpose(bd, (0, 3, 4, 1, 2, 5)).reshape(depth, N_LRU_GROUPS, LRU_GROUP, 4 * LRU_GROUP)
    b = 0.5 * b_gates.reshape(depth, 2, 2, N_LRU_GROUPS, LRU_GROUP)
    bg = jnp.transpose(b, (0, 3, 1, 2, 4)).reshape(depth, N_LRU_GROUPS, 1, 4 * LRU_GROUP)
    return wg.astype(BF16), bg


def _fft(xs):
    n = len(xs)
    if n == 1:
        return xs
    even, odd = _fft(xs[0::2]), _fft(xs[1::2])
    out = [None] * n
    for k in range(n // 2):
        (er, ei), (qr, qi) = even[k], odd[k]
        if k == 0:
            tr, ti = qr, qi
        elif 4 * k == n:
            tr, ti = qi, -qr
        else:
            wr, wi = math.cos(2 * math.pi * k / n), -math.sin(2 * math.pi * k / n)
            tr, ti = wr * qr - wi * qi, wr * qi + wi * qr
        out[k] = (er + tr, ei + ti)
        out[k + n // 2] = (er - tr, ei - ti)
    return out


def _fourier_kernel(x_ref, t_ref, cs_ref, o_ref, u_ref, p_ref, *, radix):
    q = DFT_Q
    n_half = D_FNET // LANES
    for a in range(radix):
        u = jnp.dot(t_ref[a], x_ref[a], preferred_element_type=F32)
        for part in range(2):
            for s in range(n_half):
                dst = p_ref if radix == 1 else u_ref.at[a]
                dst[part * n_half + s] = u[part * q:(part + 1) * q, s * LANES:(s + 1) * LANES]

    if radix > 1:
        def butterflies(i, _):
            s = i // (q // SUBLANES)
            r = pl.multiple_of((i % (q // SUBLANES)) * SUBLANES, SUBLANES)
            xs = [(u_ref[a, s, pl.ds(r, SUBLANES), :], u_ref[a, n_half + s, pl.ds(r, SUBLANES), :])
                  for a in range(radix)]
            for c, (pr, pi) in enumerate(_fft(xs)):
                p_ref[s, pl.ds(c * q + r, SUBLANES), :] = pr
                p_ref[n_half + s, pl.ds(c * q + r, SUBLANES), :] = pi
            return 0

        lax.fori_loop(0, n_half * (q // SUBLANES), butterflies, 0)

    for c in range(radix):
        rows = slice(c * q, (c + 1) * q)
        p = jnp.concatenate([p_ref[s, rows, :] for s in range(2 * n_half)], axis=1).astype(BF16)
        o_ref[rows, :] = jnp.dot(p, cs_ref[...], preferred_element_type=F32).astype(BF16)


def _fourier(f, tables, cs, *, batch, seq):
    radix = seq // DFT_Q
    n_slab = 2 * (D_FNET // LANES)
    return pl.pallas_call(
        functools.partial(_fourier_kernel, radix=radix),
        grid=(batch,),
        in_specs=[
            pl.BlockSpec((None, radix, DFT_Q, D_FNET), lambda b: (b, 0, 0, 0)),
            _resident((radix, 2 * DFT_Q, DFT_Q), lambda b: (0, 0, 0)),
            _resident((2 * D_FNET, D_FNET), lambda b: (0, 0)),
        ],
        out_specs=pl.BlockSpec((seq, D_FNET), lambda b: (b, 0)),
        out_shape=jax.ShapeDtypeStruct((batch * seq, D_FNET), BF16),
        scratch_shapes=[pltpu.VMEM((radix, n_slab, DFT_Q, LANES), F32),
                        pltpu.VMEM((n_slab, seq, LANES), F32)],
        compiler_params=_cparams(("arbitrary",)),
        name="fourier",
    )(f.reshape(batch, radix, DFT_Q, D_FNET), tables, cs)


def _position_dft_tables(seq):
    radix = seq // DFT_Q
    scale = 1.0 / math.sqrt(seq * HEAD_DIM)
    a = jnp.arange(radix, dtype=jnp.int32)[:, None, None]
    d = jnp.arange(DFT_Q, dtype=jnp.int32)[None, :, None]
    b = jnp.arange(DFT_Q, dtype=jnp.int32)[None, None, :]
    ang = ((d * (a + radix * b)) % seq).astype(F32) * (2.0 * math.pi / seq)
    return jnp.concatenate([jnp.cos(ang) * scale, -jnp.sin(ang) * scale], axis=1).astype(BF16)


def _channel_dft_matrix():
    c = jnp.arange(D_FNET, dtype=jnp.int32)
    same_group = (c[:, None] // HEAD_DIM) == (c[None, :] // HEAD_DIM)
    ang = (((c[:, None] % HEAD_DIM) * (c[None, :] % HEAD_DIM)) % HEAD_DIM).astype(F32) * (
        2.0 * math.pi / HEAD_DIM)
    cs = jnp.concatenate([jnp.where(same_group, jnp.cos(ang), 0.0),
                          jnp.where(same_group, jnp.sin(ang), 0.0)], axis=0)
    return cs.astype(BF16)


def _pos_table_kernel(f_ref, o_ref):
    j = lax.broadcasted_iota(jnp.int32, (GRID_W, D_MODEL // 4), 0).astype(F32)
    e = j * f_ref[...]
    o_ref[...] = jnp.concatenate([jnp.sin(e), jnp.cos(e)], axis=1)


def _pos_table():
    q = D_MODEL // 4
    freqs = 1.0 / (10000.0 ** (jnp.arange(q, dtype=F32) / q))
    return pl.pallas_call(
        _pos_table_kernel,
        out_shape=jax.ShapeDtypeStruct((GRID_W, D_MODEL // 2), F32),
        name="pos_table",
    )(freqs.reshape(1, q))


def kernel(x, c, ctx, c_ctx, w_mod, b_mod, norm_g, ffn_w_gu, ffn_w_down, w_in, w_out, conv_w, conv_b,
           lru_w_gates, lru_b_gates, lru_lambda, gmlp_ws, gmlp_bs, final_norm_g):
    batch, seq, _ = x.shape
    ctx_len = ctx.shape[1]
    depth = w_mod.shape[0]
    assert batch < MOD_ROWS and seq % ROW_TILE == 0 and (batch * ctx_len) % ROW_TILE == 0
    assert seq % (N_SEG * LRU_CHUNK) == 0 and ctx_len % (N_SEG * SUBLANES) == 0
    assert LRU_CHUNK % (ctx_len // N_SEG) == 0 or (ctx_len // N_SEG) % LRU_CHUNK == 0
    radix = seq // DFT_Q
    assert seq == radix * DFT_Q and radix & (radix - 1) == 0 and ROW_TILE % radix == 0
    assert ctx_len == DFT_Q
    ctx_row = batch
    lat = dict(tiles_per_seq=seq // ROW_TILE)
    cx = dict(fixed_row=ctx_row)

    cvec = jnp.zeros((MOD_ROWS, D_MODEL), F32).at[:batch].set(c).at[ctx_row].set(c_ctx)
    mod = _adaln(cvec, w_mod, b_mod).reshape(depth, MOD_ROWS, N_MOD, D_MODEL)

    wgu = ffn_w_gu.astype(BF16)
    wd = ffn_w_down.astype(BF16)
    w_in_b = w_in.astype(BF16)
    w_out_b = w_out.astype(BF16)
    norm_g4 = norm_g.reshape(depth, 3, 1, D_MODEL)
    ws = gmlp_ws.astype(BF16).reshape(depth, N_GMLP_GROUPS * GMLP_CHUNK, GMLP_CHUNK)
    bs = jnp.repeat(jnp.swapaxes(gmlp_bs, 1, 2), HEAD_DIM, axis=2)
    wg, bg = _lru_gate_weights(lru_w_gates, lru_b_gates)
    conv_b3 = conv_b.reshape(depth, 1, D_LRU)
    cs = _channel_dft_matrix()
    tab_lat = _position_dft_tables(seq)
    tab_ctx = _position_dft_tables(ctx_len)
    pos_tab = _pos_table()

    h = x.reshape(batch * seq, D_MODEL)
    hc = ctx.reshape(batch * ctx_len, D_MODEL)
    zero_state = jnp.zeros((batch, 2, D_LRU), F32)

    for l in range(depth):
        last = l == depth - 1
        ffn = functools.partial(_ffn, mod=mod, norm_g=norm_g4, wgu=wgu, wd=wd, layer=l)
        lru = functools.partial(_lru, conv_w=conv_w, conv_b=conv_b3, wg=wg, bg=bg, lam=lru_lambda,
                                layer=l, batch=batch)

        h, xa, ga, gu, gv, f = ffn(h, which=0, pos_tab=pos_tab if l == 0 else None, batch=batch,
                                   proj=(w_in_b, radix), **lat)
        hc, xa_c, ga_c, gu_c, gv_c, f_c = ffn(hc, which=0, proj=(w_in_b, 1), **cx)

        ya_c, state_c = lru(xa_c, ga_c, h0=zero_state, seq=ctx_len)
        ya, _ = lru(xa, ga, h0=state_c, seq=seq)
        yc = _fourier(f, tab_lat, cs, batch=batch, seq=seq)

        h = ffn(h, which=1, mix=(ya, gu, gv, yc, ws, bs, w_out_b),
                final_g=final_norm_g if last else None, **lat)
        if not last:
            yc_c = _fourier(f_c, tab_ctx, cs, batch=batch, seq=ctx_len)
            hc = ffn(hc, which=1, mix=(ya_c, gu_c, gv_c, yc_c, ws, bs, w_out_b), **cx)
    return h.reshape(batch, seq, D_MODEL)
```

```python
import functools
import math

import jax
import jax.numpy as jnp
from jax import lax
from jax.experimental import pallas as pl
from jax.experimental.pallas import tpu as pltpu

D_MODEL = 1024
GRID_W = 64
HEAD_DIM = 64
D_FF = ((8 * D_MODEL // 3 + 127) // 128) * 128
D_LRU = D_MODEL // 2
N_LRU_HEADS = D_LRU // HEAD_DIM
D_GMLP = D_MODEL // 4
N_GMLP_GROUPS = D_GMLP // HEAD_DIM
GMLP_CHUNK = 128
D_FNET = D_MODEL // 4
N_FNET_GROUPS = D_FNET // HEAD_DIM
D_IN = 2 * D_LRU + 2 * D_GMLP + D_FNET
CONV_W = 4
LRU_C = 8.0
N_MOD = 9
EPS = 1e-6

LANES = 128
SUBLANES = 8
VMEM_LIMIT_BYTES = 56 * 1024 * 1024

ROW_TILE = 1024
FF_CHUNK = 256
MOD_ROWS = 16
LRU_GROUP = LANES
N_LRU_GROUPS = D_LRU // LRU_GROUP
N_SEG = SUBLANES
LRU_CHUNK = 64
DFT_Q = 256

BF16 = jnp.bfloat16
F32 = jnp.float32
F32_TINY = float(jnp.finfo(jnp.float32).tiny)
LOG2_E = math.log2(math.e)


def _cparams(sem):
    return pltpu.CompilerParams(dimension_semantics=sem, vmem_limit_bytes=VMEM_LIMIT_BYTES)


def _resident(shape, index_map):
    return pl.BlockSpec(shape, index_map, pipeline_mode=pl.Buffered(1))


def _adaln_kernel(c_ref, w_ref, b_ref, o_ref):
    c = c_ref[...]
    s = (c * jax.nn.sigmoid(c)).astype(BF16)
    o_ref[...] = jnp.dot(s, w_ref[...].astype(BF16), preferred_element_type=F32) + b_ref[...]


def _adaln(cvec, w_mod, b_mod):
    depth = w_mod.shape[0]
    return pl.pallas_call(
        _adaln_kernel,
        grid=(depth, N_MOD),
        in_specs=[
            pl.BlockSpec((MOD_ROWS, D_MODEL), lambda l, j: (0, 0)),
            pl.BlockSpec((None, D_MODEL, D_MODEL), lambda l, j: (l, 0, j)),
            pl.BlockSpec((None, 1, D_MODEL), lambda l, j: (l, 0, j)),
        ],
        out_specs=pl.BlockSpec((None, MOD_ROWS, D_MODEL), lambda l, j: (l, 0, j)),
        out_shape=jax.ShapeDtypeStruct((depth, MOD_ROWS, N_MOD * D_MODEL), F32),
        compiler_params=_cparams(("arbitrary", "arbitrary")),
        name="adaln",
    )(cvec, w_mod, b_mod.reshape(depth, 1, N_MOD * D_MODEL))


def _modulated_norm(h, g, shift, scale):
    y = h * lax.rsqrt(jnp.mean(h * h, axis=-1, keepdims=True) + EPS)
    return ((y * g) * (1.0 + scale) + shift).astype(BF16)


def _mod_spec(layer, tiles_per_seq, fixed_row):
    if fixed_row is not None:
        index = lambda i: (layer, fixed_row, 0, 0)
    else:
        index = lambda i: (layer, i // tiles_per_seq, 0, 0)
    return pl.BlockSpec((None, None, N_MOD, D_MODEL), index)


def _row_spec(width):
    return pl.BlockSpec((ROW_TILE, width), lambda i: (i, 0))


def _layer_resident(shape, *lead):
    zeros = (0,) * len(shape)
    return _resident((None,) * len(lead) + shape, lambda i: lead + zeros)


def _mixer_output(h, gate, ya_ref, gu_ref, gv_ref, yc_ref, ws_ref, bs_ref, wo_ref):
    tm = h.shape[0]
    half = D_GMLP // 2
    lane = lax.broadcasted_iota(jnp.int32, (GMLP_CHUNK, half), 1)
    first_group = lane < HEAD_DIM
    yb = []
    for c in range(tm // GMLP_CHUNK):
        rows = slice(c * GMLP_CHUNK, (c + 1) * GMLP_CHUNK)
        cols = []
        for p in range(2):
            m = jnp.dot(ws_ref[p * 2 * GMLP_CHUNK:(p + 1) * 2 * GMLP_CHUNK, :],
                        gv_ref[rows, p * half:(p + 1) * half], preferred_element_type=F32)
            cols.append(jnp.where(first_group, m[:GMLP_CHUNK], m[GMLP_CHUNK:]))
        mixed = jnp.concatenate(cols, axis=1) + bs_ref[...]
        yb.append((gu_ref[rows, :].astype(F32) * mixed).astype(BF16))
    yb = jnp.concatenate(yb, axis=0)
    y = jnp.dot(ya_ref[...], wo_ref[0:D_LRU, :], preferred_element_type=F32)
    y = y + jnp.dot(yb, wo_ref[D_LRU:D_LRU + D_GMLP, :], preferred_element_type=F32)
    y = y + jnp.dot(yc_ref[...], wo_ref[D_LRU + D_GMLP:, :], preferred_element_type=F32)
    return h + gate * y


def _mixer_input(h, mod_ref, g_ref, w_ref, xa_ref, ga_ref, gu_ref, gv_ref, f_ref, fs_ref, radix):
    n = _modulated_norm(h, g_ref[...], mod_ref[3:4, :], mod_ref[4:5, :])

    def proj(lo, width):
        return jnp.dot(n, w_ref[:, lo:lo + width], preferred_element_type=F32)

    xa_ref[...] = proj(0, D_LRU)
    ga_ref[...] = jax.nn.gelu(proj(D_LRU, D_LRU)).astype(BF16)
    gu_ref[...] = jax.nn.gelu(proj(2 * D_LRU, D_GMLP)).astype(BF16)
    gv_ref[...] = jax.nn.gelu(proj(2 * D_LRU + D_GMLP, D_GMLP)).astype(BF16)
    f = proj(2 * D_LRU + 2 * D_GMLP, D_FNET)
    if radix == 1:
        f_ref[...] = f.astype(BF16)
        return
    n_slabs = D_FNET // LANES
    for s in range(n_slabs):
        fs_ref[s] = f[:, s * LANES:(s + 1) * LANES]
    per_class = h.shape[0] // radix
    for a in range(radix):
        piece = [fs_ref[s, pl.ds(a, per_class, stride=radix), :] for s in range(n_slabs)]
        f_ref[a] = jnp.concatenate(piece, axis=1).astype(BF16)


def _ffn_kernel(*refs, k0, embed, mix, final_norm):
    refs = list(refs)
    take = lambda n: [refs.pop(0) for _ in range(n)]
    (h_ref,) = take(1)
    (tab_ref,) = take(1) if embed else (None,)
    (mod_ref,) = take(1)
    mix_refs = take(7) if mix else None
    g_ref, wgu_ref, wd_ref = take(3)
    (gf_ref,) = take(1) if final_norm else (None,)
    (o_ref,) = take(1)
    assert not refs

    h = h_ref[...]
    if embed:
        tm = h.shape[0]
        n_grid_rows = tm // GRID_W
        tiles_per_seq = pl.num_programs(0) // embed
        r0 = (pl.program_id(0) % tiles_per_seq) * n_grid_rows
        tab = tab_ref[...]
        row_part = jnp.concatenate(
            [jnp.broadcast_to(tab_ref[pl.ds(r0 + a, 1), :], (GRID_W, D_MODEL // 2))
             for a in range(n_grid_rows)], axis=0)
        col_part = jnp.concatenate([tab] * n_grid_rows, axis=0)
        h = h + jnp.concatenate([row_part, col_part], axis=1)
    if mix:
        h = _mixer_output(h, mod_ref[5:6, :], *mix_refs)

    shift = mod_ref[k0:k0 + 1, :]
    scale = mod_ref[k0 + 1:k0 + 2, :]
    gate = mod_ref[k0 + 2:k0 + 3, :]
    n = _modulated_norm(h, g_ref[...], shift, scale)

    acc = jnp.zeros(h.shape, F32)
    for j in range(D_FF // FF_CHUNK):
        lo = j * FF_CHUNK
        gt = jnp.dot(n, wgu_ref[:, lo:lo + FF_CHUNK], preferred_element_type=F32)
        up = jnp.dot(n, wgu_ref[:, D_FF + lo:D_FF + lo + FF_CHUNK], preferred_element_type=F32)
        a = (gt * jax.nn.sigmoid(gt) * up).astype(BF16)
        acc = acc + jnp.dot(a, wd_ref[lo:lo + FF_CHUNK, :], preferred_element_type=F32)
    out = h + 0.5 * gate * acc
    if final_norm:
        out = out * lax.rsqrt(jnp.mean(out * out, axis=-1, keepdims=True) + EPS) * gf_ref[...]
    o_ref[...] = out


def _ffn(h, mod, norm_g, wgu, wd, *, layer, which, tiles_per_seq=None, fixed_row=None,
         pos_tab=None, batch=None, mix=None, final_g=None):
    rows = h.shape[0]
    in_specs = [_row_spec(D_MODEL)]
    args = [h]
    if pos_tab is not None:
        in_specs.append(_resident((GRID_W, D_MODEL // 2), lambda i: (0, 0)))
        args.append(pos_tab)
    in_specs.append(_mod_spec(layer, tiles_per_seq, fixed_row))
    args.append(mod)
    if mix is not None:
        ya, gu, gv, yc, ws, bs, w_out = mix
        in_specs += [
            _row_spec(D_LRU), _row_spec(D_GMLP), _row_spec(D_GMLP), _row_spec(D_FNET),
            _layer_resident((N_GMLP_GROUPS * GMLP_CHUNK, GMLP_CHUNK), layer),
            _layer_resident((GMLP_CHUNK, D_GMLP), layer),
            _layer_resident((D_MODEL, D_MODEL), layer),
        ]
        args += [ya, gu, gv, yc, ws, bs, w_out]
    in_specs += [
        _layer_resident((1, D_MODEL), layer, 2 * which),
        _layer_resident((D_MODEL, 2 * D_FF), layer, which),
        _layer_resident((D_FF, D_MODEL), layer, which),
    ]
    args += [norm_g, wgu, wd]
    if final_g is not None:
        in_specs.append(_resident((1, D_MODEL), lambda i: (0, 0)))
        args.append(final_g.reshape(1, D_MODEL))
    kern = functools.partial(_ffn_kernel, k0=6 * which, embed=batch if pos_tab is not None else 0,
                             mix=mix is not None, final_norm=final_g is not None)
    return pl.pallas_call(
        kern,
        grid=(rows // ROW_TILE,),
        in_specs=in_specs,
        out_specs=_row_spec(D_MODEL),
        out_shape=jax.ShapeDtypeStruct((rows, D_MODEL), F32),
        compiler_params=_cparams(("arbitrary",)),
        name="ffn",
    )(*args)


def _mix_in_kernel(h_ref, mod_ref, g_ref, w_ref, xa_ref, ga_ref, gu_ref, gv_ref, f_ref, *scratch,
                   radix):
    fs_ref = scratch[0] if scratch else None
    _mixer_input(h_ref[...], mod_ref, g_ref, w_ref, xa_ref, ga_ref, gu_ref, gv_ref, f_ref, fs_ref,
                 radix)


def _mix_in(h, mod, norm_g, w_in, *, layer, radix, tiles_per_seq=None, fixed_row=None):
    rows = h.shape[0]
    out_specs = [_row_spec(D_LRU), _row_spec(D_LRU), _row_spec(D_GMLP), _row_spec(D_GMLP)]
    out_shape = [
        jax.ShapeDtypeStruct((rows, D_LRU), F32),
        jax.ShapeDtypeStruct((rows, D_LRU), BF16),
        jax.ShapeDtypeStruct((rows, D_GMLP), BF16),
        jax.ShapeDtypeStruct((rows, D_GMLP), BF16),
    ]
    scratch = []
    if radix == 1:
        out_specs.append(_row_spec(D_FNET))
        out_shape.append(jax.ShapeDtypeStruct((rows, D_FNET), BF16))
    else:
        per_class = ROW_TILE // radix
        out_specs.append(pl.BlockSpec(
            (None, radix, per_class, D_FNET),
            lambda i: (i // tiles_per_seq, 0, i % tiles_per_seq, 0)))
        out_shape.append(jax.ShapeDtypeStruct(
            (rows // (tiles_per_seq * ROW_TILE), radix, tiles_per_seq * per_class, D_FNET), BF16))
        scratch.append(pltpu.VMEM((D_FNET // LANES, ROW_TILE, LANES), F32))
    return pl.pallas_call(
        functools.partial(_mix_in_kernel, radix=radix),
        grid=(rows // ROW_TILE,),
        in_specs=[
            _row_spec(D_MODEL),
            _mod_spec(layer, tiles_per_seq, fixed_row),
            _layer_resident((1, D_MODEL), layer, 1),
            _layer_resident((D_MODEL, D_IN), layer),
        ],
        out_specs=out_specs,
        out_shape=out_shape,
        scratch_shapes=scratch,
        compiler_params=_cparams(("arbitrary",)),
        name="mix_in",
    )(h, mod, norm_g, w_in)


def _lru_coefficients(g, xh, c_half):
    t_r = jnp.tanh(g[:, :LRU_GROUP])
    t_i = jnp.tanh(g[:, LRU_GROUP:])
    m = c_half * t_r + c_half
    a = jnp.exp2(m * (-LOG2_E))
    z = jnp.tanh(m) * (a * a + 1.0)
    root = z * lax.rsqrt(jnp.maximum(z, F32_TINY))
    b = (root * xh) * (t_i + 1.0)
    return a, b


def _lru_kernel(xa_ref, ga_ref, cw_ref, cb_ref, wg_ref, bg_ref, lam_ref, h0_ref,
                ya_ref, st_ref, xin_ref, xc_ref, pf_ref, hf_ref, pb_ref, hb_ref, *, seq):
    seg = seq // N_SEG
    chunk = min(LRU_CHUNK, seg)
    blk = chunk * N_SEG
    n_blk = seg // chunk
    v = N_SEG

    for j in range(N_SEG):
        xin_ref[pl.ds(2 * v + j, seg, stride=N_SEG), :] = xa_ref[j * seg:(j + 1) * seg, :]
    zrow = jnp.zeros((1, LRU_GROUP), F32)
    for k in (0, 1):
        tail = xin_ref[(seg + k) * v:(seg + k + 1) * v, :]
        xin_ref[k * v:(k + 1) * v, :] = jnp.concatenate([zrow, tail[:N_SEG - 1]], axis=0)
    head = xin_ref[2 * v:3 * v, :]
    xin_ref[(seg + 2) * v:(seg + 3) * v, :] = jnp.concatenate([head[1:], zrow], axis=0)

    cw = 0.5 * cw_ref[...]
    cb = 0.5 * cb_ref[...]
    for c in range(n_blk):
        acc = cb
        for k in range(CONV_W):
            acc = acc + xin_ref[c * blk + k * v:c * blk + k * v + blk, :] * cw[k:k + 1, :]
        xc_ref[c * blk:(c + 1) * blk, :] = acc

    c_half = (0.5 * LRU_C) * jax.nn.softplus(-lam_ref[...])
    bg = bg_ref[...]

    def block(c, d, state, p_ref, h_ref):
        rows = slice(c * blk, (c + 1) * blk)
        xh = xc_ref[rows, :]
        cols = slice(2 * d * LRU_GROUP, (2 * d + 2) * LRU_GROUP)
        g = jnp.dot(xh.astype(BF16), wg_ref[:, cols], preferred_element_type=F32) + bg[:, cols]
        a, b = _lru_coefficients(g, xh, c_half[d:d + 1, :])
        p, h = state
        ps, hs = [], []
        steps = range(chunk) if d == 0 else reversed(range(chunk))
        for s in steps:
            a_s = a[s * v:(s + 1) * v, :]
            p = a_s * p
            h = a_s * h + b[s * v:(s + 1) * v, :]
            ps.append(p)
            hs.append(h)
        if d == 1:
            ps, hs = ps[::-1], hs[::-1]
        p_ref[rows, :] = jnp.concatenate(ps, axis=0)
        h_ref[rows, :] = jnp.concatenate(hs, axis=0)
        return p, h

    one = jnp.ones((N_SEG, LRU_GROUP), F32)
    zero = jnp.zeros((N_SEG, LRU_GROUP), F32)
    fwd = bwd = (one, zero)
    for c in range(n_blk):
        fwd = block(c, 0, fwd, pf_ref, hf_ref)
        bwd = block(n_blk - 1 - c, 1, bwd, pb_ref, hb_ref)
    (pf, hf), (pb, hb) = fwd, bwd

    h0 = h0_ref[...]
    c = h0[0:1, :]
    rows = []
    for j in range(N_SEG):
        rows.append(c)
        c = hf[j:j + 1, :] + pf[j:j + 1, :] * c
    carry_f = jnp.concatenate(rows * chunk, axis=0)
    final_f = c
    c = h0[1:2, :]
    rows = []
    for j in reversed(range(N_SEG)):
        rows.append(c)
        c = hb[j:j + 1, :] + pb[j:j + 1, :] * c
    carry_b = jnp.concatenate(rows[::-1] * chunk, axis=0)
    final_b = c
    st_ref[...] = jnp.concatenate([final_f, final_b], axis=0)

    for c in range(n_blk):
        rows = slice(c * blk, (c + 1) * blk)
        xin_ref[rows, :] = ((hf_ref[rows, :] + pf_ref[rows, :] * carry_f)
                            + (hb_ref[rows, :] + pb_ref[rows, :] * carry_b))
    for j in range(N_SEG):
        y = xin_ref[pl.ds(j, seg, stride=N_SEG), :]
        ga = ga_ref[j * seg:(j + 1) * seg, :].astype(F32)
        ya_ref[j * seg:(j + 1) * seg, :] = (y * ga).astype(BF16)


def _lru(xa, ga, conv_w, conv_b, wg, bg, lam, h0, *, layer, batch, seq):
    blk = lambda: pl.BlockSpec((seq, LRU_GROUP), lambda b, g: (b, g))
    grp = lambda r: pl.BlockSpec((None, r, LRU_GROUP), lambda b, g: (layer, 0, g))
    st = lambda: pl.BlockSpec((None, 2, LRU_GROUP), lambda b, g: (b, 0, g))
    seg = seq // N_SEG
    return pl.pallas_call(
        functools.partial(_lru_kernel, seq=seq),
        grid=(batch, N_LRU_GROUPS),
        in_specs=[
            blk(), blk(), grp(CONV_W), grp(1),
            pl.BlockSpec((None, None, LRU_GROUP, 4 * LRU_GROUP), lambda b, g: (layer, g, 0, 0)),
            pl.BlockSpec((None, None, 1, 4 * LRU_GROUP), lambda b, g: (layer, g, 0, 0)),
            grp(2), st(),
        ],
        out_specs=[blk(), st()],
        out_shape=[
            jax.ShapeDtypeStruct((batch * seq, D_LRU), BF16),
            jax.ShapeDtypeStruct((batch, 2, D_LRU), F32),
        ],
        scratch_shapes=[pltpu.VMEM(((seg + CONV_W - 1) * N_SEG, LRU_GROUP), F32)]
        + [pltpu.VMEM((seq, LRU_GROUP), F32) for _ in range(5)],
        compiler_params=_cparams(("arbitrary", "arbitrary")),
        name="lru",
    )(xa, ga, conv_w, conv_b, wg, bg, lam, h0)


def _lru_gate_weights(w_gates, b_gates):
    depth = w_gates.shape[0]
    heads_per_group = LRU_GROUP // HEAD_DIM
    w = w_gates.reshape(depth, 2, 2, N_LRU_GROUPS, heads_per_group, HEAD_DIM, HEAD_DIM)
    eye = jnp.eye(heads_per_group, dtype=w.dtype)
    bd = w[:, :, :, :, :, :, None, :] * eye[None, None, None, None, :, None, :, None]
    bd = bd.reshape(depth, 2, 2, N_LRU_GROUPS, LRU_GROUP, LRU_GROUP)
    wg = jnp.transpose(bd, (0, 3, 4, 1, 2, 5)).reshape(depth, N_LRU_GROUPS, LRU_GROUP, 4 * LRU_GROUP)
    b = 0.5 * b_gates.reshape(depth, 2, 2, N_LRU_GROUPS, LRU_GROUP)
    bg = jnp.transpose(b, (0, 3, 1, 2, 4)).reshape(depth, N_LRU_GROUPS, 1, 4 * LRU_GROUP)
    return wg.astype(BF16), bg


def _fft(xs):
    n = len(xs)
    if n == 1:
        return xs
    even, odd = _fft(xs[0::2]), _fft(xs[1::2])
    out = [None] * n
    for k in range(n // 2):
        (er, ei), (qr, qi) = even[k], odd[k]
        if k == 0:
            tr, ti = qr, qi
        elif 4 * k == n:
            tr, ti = qi, -qr
        else:
            wr, wi = math.cos(2 * math.pi * k / n), -math.sin(2 * math.pi * k / n)
            tr, ti = wr * qr - wi * qi, wr * qi + wi * qr
        out[k] = (er + tr, ei + ti)
        out[k + n // 2] = (er - tr, ei - ti)
    return out


def _fourier_kernel(x_ref, t_ref, cs_ref, o_ref, u_ref, p_ref, *, radix):
    q = DFT_Q
    n_half = D_FNET // LANES
    for a in range(radix):
        u = jnp.dot(t_ref[a], x_ref[a], preferred_element_type=F32)
        for part in range(2):
            for s in range(n_half):
                dst = p_ref if radix == 1 else u_ref.at[a]
                dst[part * n_half + s] = u[part * q:(part + 1) * q, s * LANES:(s + 1) * LANES]

    if radix > 1:
        def butterflies(i, _):
            s = i // (q // SUBLANES)
            r = pl.multiple_of((i % (q // SUBLANES)) * SUBLANES, SUBLANES)
            xs = [(u_ref[a, s, pl.ds(r, SUBLANES), :], u_ref[a, n_half + s, pl.ds(r, SUBLANES), :])
                  for a in range(radix)]
            for c, (pr, pi) in enumerate(_fft(xs)):
                p_ref[s, pl.ds(c * q + r, SUBLANES), :] = pr
                p_ref[n_half + s, pl.ds(c * q + r, SUBLANES), :] = pi
            return 0

        lax.fori_loop(0, n_half * (q // SUBLANES), butterflies, 0)

    for c in range(radix):
        rows = slice(c * q, (c + 1) * q)
        p = jnp.concatenate([p_ref[s, rows, :] for s in range(2 * n_half)], axis=1).astype(BF16)
        o_ref[rows, :] = jnp.dot(p, cs_ref[...], preferred_element_type=F32).astype(BF16)


def _fourier(f, tables, cs, *, batch, seq):
    radix = seq // DFT_Q
    n_slab = 2 * (D_FNET // LANES)
    return pl.pallas_call(
        functools.partial(_fourier_kernel, radix=radix),
        grid=(batch,),
        in_specs=[
            pl.BlockSpec((None, radix, DFT_Q, D_FNET), lambda b: (b, 0, 0, 0)),
            _resident((radix, 2 * DFT_Q, DFT_Q), lambda b: (0, 0, 0)),
            _resident((2 * D_FNET, D_FNET), lambda b: (0, 0)),
        ],
        out_specs=pl.BlockSpec((seq, D_FNET), lambda b: (b, 0)),
        out_shape=jax.ShapeDtypeStruct((batch * seq, D_FNET), BF16),
        scratch_shapes=[pltpu.VMEM((radix, n_slab, DFT_Q, LANES), F32),
                        pltpu.VMEM((n_slab, seq, LANES), F32)],
        compiler_params=_cparams(("arbitrary",)),
        name="fourier",
    )(f.reshape(batch, radix, DFT_Q, D_FNET), tables, cs)


def _position_dft_tables(seq):
    radix = seq // DFT_Q
    scale = 1.0 / math.sqrt(seq * HEAD_DIM)
    a = jnp.arange(radix, dtype=jnp.int32)[:, None, None]
    d = jnp.arange(DFT_Q, dtype=jnp.int32)[None, :, None]
    b = jnp.arange(DFT_Q, dtype=jnp.int32)[None, None, :]
    ang = ((d * (a + radix * b)) % seq).astype(F32) * (2.0 * math.pi / seq)
    return jnp.concatenate([jnp.cos(ang) * scale, -jnp.sin(ang) * scale], axis=1).astype(BF16)


def _channel_dft_matrix():
    c = jnp.arange(D_FNET, dtype=jnp.int32)
    same_group = (c[:, None] // HEAD_DIM) == (c[None, :] // HEAD_DIM)
    ang = (((c[:, None] % HEAD_DIM) * (c[None, :] % HEAD_DIM)) % HEAD_DIM).astype(F32) * (
        2.0 * math.pi / HEAD_DIM)
    cs = jnp.concatenate([jnp.where(same_group, jnp.cos(ang), 0.0),
                          jnp.where(same_group, jnp.sin(ang), 0.0)], axis=0)
    return cs.astype(BF16)


def _pos_table_kernel(f_ref, o_ref):
    j = lax.broadcasted_iota(jnp.int32, (GRID_W, D_MODEL // 4), 0).astype(F32)
    e = j * f_ref[...]
    o_ref[...] = jnp.concatenate([jnp.sin(e), jnp.cos(e)], axis=1)


def _pos_table():
    q = D_MODEL // 4
    freqs = 1.0 / (10000.0 ** (jnp.arange(q, dtype=F32) / q))
    return pl.pallas_call(
        _pos_table_kernel,
        out_shape=jax.ShapeDtypeStruct((GRID_W, D_MODEL // 2), F32),
        name="pos_table",
    )(freqs.reshape(1, q))


def kernel(x, c, ctx, c_ctx, w_mod, b_mod, norm_g, ffn_w_gu, ffn_w_down, w_in, w_out, conv_w, conv_b,
           lru_w_gates, lru_b_gates, lru_lambda, gmlp_ws, gmlp_bs, final_norm_g):
    batch, seq, _ = x.shape
    ctx_len = ctx.shape[1]
    depth = w_mod.shape[0]
    assert batch < MOD_ROWS and seq % ROW_TILE == 0 and (batch * ctx_len) % ROW_TILE == 0
    assert seq % (N_SEG * LRU_CHUNK) == 0 and ctx_len % (N_SEG * SUBLANES) == 0
    assert LRU_CHUNK % (ctx_len // N_SEG) == 0 or (ctx_len // N_SEG) % LRU_CHUNK == 0
    radix = seq // DFT_Q
    assert seq == radix * DFT_Q and radix & (radix - 1) == 0 and ROW_TILE % radix == 0
    assert ctx_len == DFT_Q
    ctx_row = batch
    lat = dict(tiles_per_seq=seq // ROW_TILE)
    cx = dict(fixed_row=ctx_row)

    cvec = jnp.zeros((MOD_ROWS, D_MODEL), F32).at[:batch].set(c).at[ctx_row].set(c_ctx)
    mod = _adaln(cvec, w_mod, b_mod).reshape(depth, MOD_ROWS, N_MOD, D_MODEL)

    wgu = ffn_w_gu.astype(BF16)
    wd = ffn_w_down.astype(BF16)
    w_in_b = w_in.astype(BF16)
    w_out_b = w_out.astype(BF16)
    norm_g4 = norm_g.reshape(depth, 3, 1, D_MODEL)
    ws = gmlp_ws.astype(BF16).reshape(depth, N_GMLP_GROUPS * GMLP_CHUNK, GMLP_CHUNK)
    bs = jnp.repeat(jnp.swapaxes(gmlp_bs, 1, 2), HEAD_DIM, axis=2)
    wg, bg = _lru_gate_weights(lru_w_gates, lru_b_gates)
    conv_b3 = conv_b.reshape(depth, 1, D_LRU)
    cs = _channel_dft_matrix()
    tab_lat = _position_dft_tables(seq)
    tab_ctx = _position_dft_tables(ctx_len)
    pos_tab = _pos_table()

    h = x.reshape(batch * seq, D_MODEL)
    hc = ctx.reshape(batch * ctx_len, D_MODEL)
    zero_state = jnp.zeros((batch, 2, D_LRU), F32)

    for l in range(depth):
        last = l == depth - 1
        ffn = functools.partial(_ffn, mod=mod, norm_g=norm_g4, wgu=wgu, wd=wd, layer=l)
        lru = functools.partial(_lru, conv_w=conv_w, conv_b=conv_b3, wg=wg, bg=bg, lam=lru_lambda,
                                layer=l, batch=batch)

        mix_in = functools.partial(_mix_in, mod=mod, norm_g=norm_g4, w_in=w_in_b, layer=l)

        h = ffn(h, which=0, pos_tab=pos_tab if l == 0 else None, batch=batch, **lat)
        hc = ffn(hc, which=0, **cx)

        xa_c, ga_c, gu_c, gv_c, f_c = mix_in(hc, radix=1, **cx)
        xa, ga, gu, gv, f = mix_in(h, radix=radix, **lat)
        ya_c, state_c = lru(xa_c, ga_c, h0=zero_state, seq=ctx_len)
        ya, _ = lru(xa, ga, h0=state_c, seq=seq)
        yc = _fourier(f, tab_lat, cs, batch=batch, seq=seq)

        h = ffn(h, which=1, mix=(ya, gu, gv, yc, ws, bs, w_out_b),
                final_g=final_norm_g if last else None, **lat)
        if not last:
            yc_c = _fourier(f_c, tab_ctx, cs, batch=batch, seq=ctx_len)
            hc = ffn(hc, which=1, mix=(ya_c, gu_c, gv_c, yc_c, ws, bs, w_out_b), **cx)
    return h.reshape(batch, seq, D_MODEL)
```

```python
import functools
import math

import jax
import jax.numpy as jnp
from jax import lax
from jax.experimental import pallas as pl
from jax.experimental.pallas import tpu as pltpu

D_MODEL = 1024
GRID_W = 64
HEAD_DIM = 64
D_FF = ((8 * D_MODEL // 3 + 127) // 128) * 128
D_LRU = D_MODEL // 2
N_LRU_HEADS = D_LRU // HEAD_DIM
D_GMLP = D_MODEL // 4
N_GMLP_GROUPS = D_GMLP // HEAD_DIM
GMLP_CHUNK = 128
D_FNET = D_MODEL // 4
N_FNET_GROUPS = D_FNET // HEAD_DIM
D_IN = 2 * D_LRU + 2 * D_GMLP + D_FNET
CONV_W = 4
LRU_C = 8.0
N_MOD = 9
EPS = 1e-6

LANES = 128
SUBLANES = 8
VMEM_LIMIT_BYTES = 56 * 1024 * 1024

ROW_TILE = 1024
FF_CHUNK = 256
MOD_ROWS = 16
LRU_GROUP = LANES
N_LRU_GROUPS = D_LRU // LRU_GROUP
N_SEG = SUBLANES
LRU_CHUNK = 64
PROJ_ROWS = 512
DFT_Q = 256

COL_GA = D_LRU
COL_UV = 2 * D_LRU
COL_F = 2 * D_LRU + 2 * D_GMLP

BF16 = jnp.bfloat16
F32 = jnp.float32
F32_TINY = float(jnp.finfo(jnp.float32).tiny)
LOG2_E = math.log2(math.e)


def _cparams(sem):
    return pltpu.CompilerParams(dimension_semantics=sem, vmem_limit_bytes=VMEM_LIMIT_BYTES)


def _resident(shape, index_map):
    return pl.BlockSpec(shape, index_map, pipeline_mode=pl.Buffered(1))


def _adaln_kernel(c_ref, w_ref, b_ref, o_ref):
    c = c_ref[...]
    s = (c * jax.nn.sigmoid(c)).astype(BF16)
    o_ref[...] = jnp.dot(s, w_ref[...].astype(BF16), preferred_element_type=F32) + b_ref[...]


def _adaln(cvec, w_mod, b_mod):
    depth = w_mod.shape[0]
    return pl.pallas_call(
        _adaln_kernel,
        grid=(depth, N_MOD),
        in_specs=[
            pl.BlockSpec((MOD_ROWS, D_MODEL), lambda l, j: (0, 0)),
            pl.BlockSpec((None, D_MODEL, D_MODEL), lambda l, j: (l, 0, j)),
            pl.BlockSpec((None, 1, D_MODEL), lambda l, j: (l, 0, j)),
        ],
        out_specs=pl.BlockSpec((None, MOD_ROWS, D_MODEL), lambda l, j: (l, 0, j)),
        out_shape=jax.ShapeDtypeStruct((depth, MOD_ROWS, N_MOD * D_MODEL), F32),
        compiler_params=_cparams(("arbitrary", "arbitrary")),
        name="adaln",
    )(cvec, w_mod, b_mod.reshape(depth, 1, N_MOD * D_MODEL))


def _modulated_norm(h, g, shift, scale):
    y = h * lax.rsqrt(jnp.mean(h * h, axis=-1, keepdims=True) + EPS)
    return ((y * g) * (1.0 + scale) + shift).astype(BF16)


def _mod_spec(layer, tiles_per_seq, fixed_row):
    if fixed_row is not None:
        index = lambda i: (layer, fixed_row, 0, 0)
    else:
        index = lambda i: (layer, i // tiles_per_seq, 0, 0)
    return pl.BlockSpec((None, None, N_MOD, D_MODEL), index)


def _row_spec(width):
    return pl.BlockSpec((ROW_TILE, width), lambda i: (i, 0))


def _layer_resident(shape, *lead):
    zeros = (0,) * len(shape)
    return _resident((None,) * len(lead) + shape, lambda i: lead + zeros)


def _w_in_cols(layer, first_col, width):
    assert first_col % width == 0
    return _resident((None, D_MODEL, width), lambda i: (layer, 0, first_col // width))


def _mixer_output(h, gate, n_ref, ya_ref, yc_ref, wuv_ref, ws_ref, bs_ref, wo_ref):
    tm = h.shape[0]
    half = D_GMLP // 2
    lane = lax.broadcasted_iota(jnp.int32, (GMLP_CHUNK, half), 1)
    first_group = lane < HEAD_DIM
    guv = jax.nn.gelu(jnp.dot(n_ref[...], wuv_ref[...], preferred_element_type=F32))
    gu = guv[:, :D_GMLP]
    gv = guv[:, D_GMLP:].astype(BF16)
    yb = []
    for c in range(tm // GMLP_CHUNK):
        rows = slice(c * GMLP_CHUNK, (c + 1) * GMLP_CHUNK)
        cols = []
        for p in range(2):
            m = jnp.dot(ws_ref[p * 2 * GMLP_CHUNK:(p + 1) * 2 * GMLP_CHUNK, :],
                        gv[rows, p * half:(p + 1) * half], preferred_element_type=F32)
            cols.append(jnp.where(first_group, m[:GMLP_CHUNK], m[GMLP_CHUNK:]))
        mixed = jnp.concatenate(cols, axis=1) + bs_ref[...]
        yb.append((gu[rows, :] * mixed).astype(BF16))
    yb = jnp.concatenate(yb, axis=0)
    y = jnp.dot(ya_ref[...], wo_ref[0:D_LRU, :], preferred_element_type=F32)
    y = y + jnp.dot(yb, wo_ref[D_LRU:D_LRU + D_GMLP, :], preferred_element_type=F32)
    y = y + jnp.dot(yc_ref[...], wo_ref[D_LRU + D_GMLP:, :], preferred_element_type=F32)
    return h + gate * y


def _ffn_kernel(*refs, k0, embed, mix, emit_n, final_norm):
    refs = list(refs)
    take = lambda n: [refs.pop(0) for _ in range(n)]
    (h_ref,) = take(1)
    (tab_ref,) = take(1) if embed else (None,)
    (mod_ref,) = take(1)
    mix_refs = take(7) if mix else None
    g_ref, wgu_ref, wd_ref = take(3)
    (gmix_ref,) = take(1) if emit_n else (None,)
    (gf_ref,) = take(1) if final_norm else (None,)
    (o_ref,) = take(1)
    (n_ref,) = take(1) if emit_n else (None,)
    assert not refs

    h = h_ref[...]
    if embed:
        tm = h.shape[0]
        n_grid_rows = tm // GRID_W
        tiles_per_seq = pl.num_programs(0) // embed
        r0 = (pl.program_id(0) % tiles_per_seq) * n_grid_rows
        tab = tab_ref[...]
        row_part = jnp.concatenate(
            [jnp.broadcast_to(tab_ref[pl.ds(r0 + a, 1), :], (GRID_W, D_MODEL // 2))
             for a in range(n_grid_rows)], axis=0)
        col_part = jnp.concatenate([tab] * n_grid_rows, axis=0)
        h = h + jnp.concatenate([row_part, col_part], axis=1)
    if mix:
        h = _mixer_output(h, mod_ref[5:6, :], *mix_refs)

    shift = mod_ref[k0:k0 + 1, :]
    scale = mod_ref[k0 + 1:k0 + 2, :]
    gate = mod_ref[k0 + 2:k0 + 3, :]
    n = _modulated_norm(h, g_ref[...], shift, scale)

    act = []
    for j in range(D_FF // FF_CHUNK):
        lo = j * FF_CHUNK
        gt = jnp.dot(n, wgu_ref[:, lo:lo + FF_CHUNK], preferred_element_type=F32)
        up = jnp.dot(n, wgu_ref[:, D_FF + lo:D_FF + lo + FF_CHUNK], preferred_element_type=F32)
        act.append((gt * jax.nn.sigmoid(gt) * up).astype(BF16))
    acc = jnp.dot(jnp.concatenate(act, axis=1), wd_ref[...], preferred_element_type=F32)
    out = h + 0.5 * gate * acc
    if emit_n:
        n_ref[...] = _modulated_norm(out, gmix_ref[...], mod_ref[3:4, :], mod_ref[4:5, :])
    if final_norm:
        out = out * lax.rsqrt(jnp.mean(out * out, axis=-1, keepdims=True) + EPS) * gf_ref[...]
    o_ref[...] = out


def _ffn(h, mod, norm_g, wgu, wd, *, layer, which, tiles_per_seq=None, fixed_row=None,
         pos_tab=None, batch=None, mix=None, emit_n=False, final_g=None):
    rows = h.shape[0]
    in_specs = [_row_spec(D_MODEL)]
    args = [h]
    if pos_tab is not None:
        in_specs.append(_resident((GRID_W, D_MODEL // 2), lambda i: (0, 0)))
        args.append(pos_tab)
    in_specs.append(_mod_spec(layer, tiles_per_seq, fixed_row))
    args.append(mod)
    if mix is not None:
        n, ya, yc, w_in, ws, bs, w_out = mix
        in_specs += [
            _row_spec(D_MODEL), _row_spec(D_LRU), _row_spec(D_FNET),
            _w_in_cols(layer, COL_UV, 2 * D_GMLP),
            _layer_resident((N_GMLP_GROUPS * GMLP_CHUNK, GMLP_CHUNK), layer),
            _layer_resident((GMLP_CHUNK, D_GMLP), layer),
            _layer_resident((D_MODEL, D_MODEL), layer),
        ]
        args += [n, ya, yc, w_in, ws, bs, w_out]
    in_specs += [
        _layer_resident((1, D_MODEL), layer, 2 * which),
        _layer_resident((D_MODEL, 2 * D_FF), layer, which),
        _layer_resident((D_FF, D_MODEL), layer, which),
    ]
    args += [norm_g, wgu, wd]
    out_specs = [_row_spec(D_MODEL)]
    out_shape = [jax.ShapeDtypeStruct((rows, D_MODEL), F32)]
    if emit_n:
        in_specs.append(_layer_resident((1, D_MODEL), layer, 1))
        args.append(norm_g)
        out_specs.append(_row_spec(D_MODEL))
        out_shape.append(jax.ShapeDtypeStruct((rows, D_MODEL), BF16))
    if final_g is not None:
        in_specs.append(_resident((1, D_MODEL), lambda i: (0, 0)))
        args.append(final_g.reshape(1, D_MODEL))
    kern = functools.partial(_ffn_kernel, k0=6 * which, embed=batch if pos_tab is not None else 0,
                             mix=mix is not None, emit_n=emit_n, final_norm=final_g is not None)
    out = pl.pallas_call(
        kern,
        grid=(rows // ROW_TILE,),
        in_specs=in_specs,
        out_specs=out_specs,
        out_shape=out_shape,
        compiler_params=_cparams(("arbitrary",)),
        name="ffn",
    )(*args)
    return out if emit_n else out[0]


def _lru_coefficients(g, xh, c_half):
    t_r = jnp.tanh(g[:, :LRU_GROUP])
    t_i = jnp.tanh(g[:, LRU_GROUP:])
    m = c_half * t_r + c_half
    a = jnp.exp2(m * (-LOG2_E))
    z = jnp.tanh(m) * (a * a + 1.0)
    root = z * lax.rsqrt(jnp.maximum(z, F32_TINY))
    b = (root * xh) * (t_i + 1.0)
    return a, b


def _lru_kernel(n_ref, wxa_ref, wga_ref, cw_ref, cb_ref, wg_ref, bg_ref, lam_ref, h0_ref,
                ya_ref, st_ref, xin_ref, xc_ref, pf_ref, hf_ref, pb_ref, hb_ref, ga_ref, *, seq):
    seg = seq // N_SEG
    chunk = min(LRU_CHUNK, seg)
    blk = chunk * N_SEG
    n_blk = seg // chunk
    v = N_SEG

    wx = jnp.concatenate([wxa_ref[...], wga_ref[...]], axis=1)
    proj_rows = max(seg, min(seq, PROJ_ROWS))
    for r in range(seq // proj_rows):
        rows = slice(r * proj_rows, (r + 1) * proj_rows)
        z = jnp.dot(n_ref[rows, :], wx, preferred_element_type=F32)
        ga_ref[rows, :] = jax.nn.gelu(z[:, LRU_GROUP:]).astype(BF16)
        for jj in range(proj_rows // seg):
            j = r * (proj_rows // seg) + jj
            xin_ref[pl.ds(2 * v + j, seg, stride=N_SEG), :] = z[jj * seg:(jj + 1) * seg, :LRU_GROUP]
    zrow = jnp.zeros((1, LRU_GROUP), F32)
    for k in (0, 1):
        tail = xin_ref[(seg + k) * v:(seg + k + 1) * v, :]
        xin_ref[k * v:(k + 1) * v, :] = jnp.concatenate([zrow, tail[:N_SEG - 1]], axis=0)
    head = xin_ref[2 * v:3 * v, :]
    xin_ref[(seg + 2) * v:(seg + 3) * v, :] = jnp.concatenate([head[1:], zrow], axis=0)

    cw = 0.5 * cw_ref[...]
    cb = 0.5 * cb_ref[...]
    for c in range(n_blk):
        acc = cb
        for k in range(CONV_W):
            acc = acc + xin_ref[c * blk + k * v:c * blk + k * v + blk, :] * cw[k:k + 1, :]
        xc_ref[c * blk:(c + 1) * blk, :] = acc

    c_half = (0.5 * LRU_C) * jax.nn.softplus(-lam_ref[...])
    bg = bg_ref[...]

    def block(c, d, state, p_ref, h_ref):
        rows = slice(c * blk, (c + 1) * blk)
        xh = xc_ref[rows, :]
        cols = slice(2 * d * LRU_GROUP, (2 * d + 2) * LRU_GROUP)
        g = jnp.dot(xh.astype(BF16), wg_ref[:, cols], preferred_element_type=F32) + bg[:, cols]
        a, b = _lru_coefficients(g, xh, c_half[d:d + 1, :])
        p, h = state
        ps, hs = [], []
        steps = range(chunk) if d == 0 else reversed(range(chunk))
        for s in steps:
            a_s = a[s * v:(s + 1) * v, :]
            p = a_s * p
            h = a_s * h + b[s * v:(s + 1) * v, :]
            ps.append(p)
            hs.append(h)
        if d == 1:
            ps, hs = ps[::-1], hs[::-1]
        p_ref[rows, :] = jnp.concatenate(ps, axis=0)
        h_ref[rows, :] = jnp.concatenate(hs, axis=0)
        return p, h

    one = jnp.ones((N_SEG, LRU_GROUP), F32)
    zero = jnp.zeros((N_SEG, LRU_GROUP), F32)
    fwd = bwd = (one, zero)
    for c in range(n_blk):
        fwd = block(c, 0, fwd, pf_ref, hf_ref)
        bwd = block(n_blk - 1 - c, 1, bwd, pb_ref, hb_ref)
    (pf, hf), (pb, hb) = fwd, bwd

    h0 = h0_ref[...]
    c = h0[0:1, :]
    rows = []
    for j in range(N_SEG):
        rows.append(c)
        c = hf[j:j + 1, :] + pf[j:j + 1, :] * c
    carry_f = jnp.concatenate(rows * chunk, axis=0)
    final_f = c
    c = h0[1:2, :]
    rows = []
    for j in reversed(range(N_SEG)):
        rows.append(c)
        c = hb[j:j + 1, :] + pb[j:j + 1, :] * c
    carry_b = jnp.concatenate(rows[::-1] * chunk, axis=0)
    final_b = c
    st_ref[...] = jnp.concatenate([final_f, final_b], axis=0)

    for c in range(n_blk):
        rows = slice(c * blk, (c + 1) * blk)
        xin_ref[rows, :] = ((hf_ref[rows, :] + pf_ref[rows, :] * carry_f)
                            + (hb_ref[rows, :] + pb_ref[rows, :] * carry_b))
    for j in range(N_SEG):
        y = xin_ref[pl.ds(j, seg, stride=N_SEG), :]
        ga = ga_ref[j * seg:(j + 1) * seg, :].astype(F32)
        ya_ref[j * seg:(j + 1) * seg, :] = (y * ga).astype(BF16)


def _lru(n, w_in, conv_w, conv_b, wg, bg, lam, h0, *, layer, batch, seq):
    grp = lambda r: pl.BlockSpec((None, r, LRU_GROUP), lambda b, g: (layer, 0, g))
    st = lambda: pl.BlockSpec((None, 2, LRU_GROUP), lambda b, g: (b, 0, g))
    w_cols = lambda first: pl.BlockSpec((None, D_MODEL, LRU_GROUP),
                                        lambda b, g: (layer, 0, first // LRU_GROUP + g))
    seg = seq // N_SEG
    return pl.pallas_call(
        functools.partial(_lru_kernel, seq=seq),
        grid=(batch, N_LRU_GROUPS),
        in_specs=[
            pl.BlockSpec((seq, D_MODEL), lambda b, g: (b, 0)),
            w_cols(0), w_cols(COL_GA), grp(CONV_W), grp(1),
            pl.BlockSpec((None, None, LRU_GROUP, 4 * LRU_GROUP), lambda b, g: (layer, g, 0, 0)),
            pl.BlockSpec((None, None, 1, 4 * LRU_GROUP), lambda b, g: (layer, g, 0, 0)),
            grp(2), st(),
        ],
        out_specs=[pl.BlockSpec((seq, LRU_GROUP), lambda b, g: (b, g)), st()],
        out_shape=[
            jax.ShapeDtypeStruct((batch * seq, D_LRU), BF16),
            jax.ShapeDtypeStruct((batch, 2, D_LRU), F32),
        ],
        scratch_shapes=[pltpu.VMEM(((seg + CONV_W - 1) * N_SEG, LRU_GROUP), F32)]
        + [pltpu.VMEM((seq, LRU_GROUP), F32) for _ in range(5)]
        + [pltpu.VMEM((seq, LRU_GROUP), BF16)],
        compiler_params=_cparams(("arbitrary", "arbitrary")),
        name="lru",
    )(n, w_in, w_in, conv_w, conv_b, wg, bg, lam, h0)


def _lru_gate_weights(w_gates, b_gates):
    depth = w_gates.shape[0]
    heads_per_group = LRU_GROUP // HEAD_DIM
    w = w_gates.reshape(depth, 2, 2, N_LRU_GROUPS, heads_per_group, HEAD_DIM, HEAD_DIM)
    eye = jnp.eye(heads_per_group, dtype=w.dtype)
    bd = w[:, :, :, :, :, :, None, :] * eye[None, None, None, None, :, None, :, None]
    bd = bd.reshape(depth, 2, 2, N_LRU_GROUPS, LRU_GROUP, LRU_GROUP)
    wg = jnp.transpose(bd, (0, 3, 4, 1, 2, 5)).reshape(depth, N_LRU_GROUPS, LRU_GROUP, 4 * LRU_GROUP)
    b = 0.5 * b_gates.reshape(depth, 2, 2, N_LRU_GROUPS, LRU_GROUP)
    bg = jnp.transpose(b, (0, 3, 1, 2, 4)).reshape(depth, N_LRU_GROUPS, 1, 4 * LRU_GROUP)
    return wg.astype(BF16), bg


def _fft(xs):
    n = len(xs)
    if n == 1:
        return xs
    even, odd = _fft(xs[0::2]), _fft(xs[1::2])
    out = [None] * n
    for k in range(n // 2):
        (er, ei), (qr, qi) = even[k], odd[k]
        if k == 0:
            tr, ti = qr, qi
        elif 4 * k == n:
            tr, ti = qi, -qr
        else:
            wr, wi = math.cos(2 * math.pi * k / n), -math.sin(2 * math.pi * k / n)
            tr, ti = wr * qr - wi * qi, wr * qi + wi * qr
        out[k] = (er + tr, ei + ti)
        out[k + n // 2] = (er - tr, ei - ti)
    return out


def _fourier_kernel(n_ref, wf_ref, t_ref, cs_ref, o_ref, f_ref, u_ref, p_ref, *, radix):
    q = DFT_Q
    seq = radix * q
    n_half = D_FNET // LANES
    proj_rows = min(seq, PROJ_ROWS)
    for r in range(seq // proj_rows):
        rows = slice(r * proj_rows, (r + 1) * proj_rows)
        f = jnp.dot(n_ref[rows, :], wf_ref[...], preferred_element_type=F32)
        for s in range(n_half):
            f_ref[s, rows, :] = f[:, s * LANES:(s + 1) * LANES]

    for a in range(radix):
        if radix == 1:
            x = [f_ref[s] for s in range(n_half)]
        else:
            x = [f_ref[s, pl.ds(a, q, stride=radix), :] for s in range(n_half)]
        x = jnp.concatenate(x, axis=1).astype(BF16)
        u = jnp.dot(t_ref[a], x, preferred_element_type=F32)
        for part in range(2):
            for s in range(n_half):
                dst = p_ref if radix == 1 else u_ref.at[a]
                dst[part * n_half + s] = u[part * q:(part + 1) * q, s * LANES:(s + 1) * LANES]

    if radix > 1:
        def butterflies(i, _):
            s = i // (q // SUBLANES)
            r = pl.multiple_of((i % (q // SUBLANES)) * SUBLANES, SUBLANES)
            xs = [(u_ref[a, s, pl.ds(r, SUBLANES), :], u_ref[a, n_half + s, pl.ds(r, SUBLANES), :])
                  for a in range(radix)]
            for c, (pr, pi) in enumerate(_fft(xs)):
                p_ref[s, pl.ds(c * q + r, SUBLANES), :] = pr
                p_ref[n_half + s, pl.ds(c * q + r, SUBLANES), :] = pi
            return 0

        lax.fori_loop(0, n_half * (q // SUBLANES), butterflies, 0)

    for c in range(radix):
        rows = slice(c * q, (c + 1) * q)
        p = jnp.concatenate([p_ref[s, rows, :] for s in range(2 * n_half)], axis=1).astype(BF16)
        o_ref[rows, :] = jnp.dot(p, cs_ref[...], preferred_element_type=F32).astype(BF16)


def _fourier(n, w_in, tables, cs, *, layer, batch, seq):
    radix = seq // DFT_Q
    n_half = D_FNET // LANES
    return pl.pallas_call(
        functools.partial(_fourier_kernel, radix=radix),
        grid=(batch,),
        in_specs=[
            pl.BlockSpec((seq, D_MODEL), lambda b: (b, 0)),
            _w_in_cols(layer, COL_F, D_FNET),
            _resident((radix, 2 * DFT_Q, DFT_Q), lambda b: (0, 0, 0)),
            _resident((2 * D_FNET, D_FNET), lambda b: (0, 0)),
        ],
        out_specs=pl.BlockSpec((seq, D_FNET), lambda b: (b, 0)),
        out_shape=jax.ShapeDtypeStruct((batch * seq, D_FNET), BF16),
        scratch_shapes=[pltpu.VMEM((n_half, seq, LANES), F32),
                        pltpu.VMEM((radix, 2 * n_half, DFT_Q, LANES), F32),
                        pltpu.VMEM((2 * n_half, seq, LANES), F32)],
        compiler_params=_cparams(("arbitrary",)),
        name="fourier",
    )(n, w_in, tables, cs)


def _position_dft_tables(seq):
    radix = seq // DFT_Q
    scale = 1.0 / math.sqrt(seq * HEAD_DIM)
    a = jnp.arange(radix, dtype=jnp.int32)[:, None, None]
    d = jnp.arange(DFT_Q, dtype=jnp.int32)[None, :, None]
    b = jnp.arange(DFT_Q, dtype=jnp.int32)[None, None, :]
    ang = ((d * (a + radix * b)) % seq).astype(F32) * (2.0 * math.pi / seq)
    return jnp.concatenate([jnp.cos(ang) * scale, -jnp.sin(ang) * scale], axis=1).astype(BF16)


def _channel_dft_matrix():
    c = jnp.arange(D_FNET, dtype=jnp.int32)
    same_group = (c[:, None] // HEAD_DIM) == (c[None, :] // HEAD_DIM)
    ang = (((c[:, None] % HEAD_DIM) * (c[None, :] % HEAD_DIM)) % HEAD_DIM).astype(F32) * (
        2.0 * math.pi / HEAD_DIM)
    cs = jnp.concatenate([jnp.where(same_group, jnp.cos(ang), 0.0),
                          jnp.where(same_group, jnp.sin(ang), 0.0)], axis=0)
    return cs.astype(BF16)


def _pos_table_kernel(f_ref, o_ref):
    j = lax.broadcasted_iota(jnp.int32, (GRID_W, D_MODEL // 4), 0).astype(F32)
    e = j * f_ref[...]
    o_ref[...] = jnp.concatenate([jnp.sin(e), jnp.cos(e)], axis=1)


def _pos_table():
    q = D_MODEL // 4
    freqs = 1.0 / (10000.0 ** (jnp.arange(q, dtype=F32) / q))
    return pl.pallas_call(
        _pos_table_kernel,
        out_shape=jax.ShapeDtypeStruct((GRID_W, D_MODEL // 2), F32),
        name="pos_table",
    )(freqs.reshape(1, q))


def kernel(x, c, ctx, c_ctx, w_mod, b_mod, norm_g, ffn_w_gu, ffn_w_down, w_in, w_out, conv_w, conv_b,
           lru_w_gates, lru_b_gates, lru_lambda, gmlp_ws, gmlp_bs, final_norm_g):
    batch, seq, _ = x.shape
    ctx_len = ctx.shape[1]
    depth = w_mod.shape[0]
    assert batch < MOD_ROWS and seq % ROW_TILE == 0 and (batch * ctx_len) % ROW_TILE == 0
    assert seq % (N_SEG * LRU_CHUNK) == 0 and ctx_len % (N_SEG * SUBLANES) == 0
    assert LRU_CHUNK % (ctx_len // N_SEG) == 0 or (ctx_len // N_SEG) % LRU_CHUNK == 0
    radix = seq // DFT_Q
    assert seq == radix * DFT_Q and radix & (radix - 1) == 0 and seq % PROJ_ROWS == 0
    assert ctx_len == DFT_Q
    ctx_row = batch
    lat = dict(tiles_per_seq=seq // ROW_TILE)
    cx = dict(fixed_row=ctx_row)

    cvec = jnp.zeros((MOD_ROWS, D_MODEL), F32).at[:batch].set(c).at[ctx_row].set(c_ctx)
    mod = _adaln(cvec, w_mod, b_mod).reshape(depth, MOD_ROWS, N_MOD, D_MODEL)

    wgu = ffn_w_gu.astype(BF16)
    wd = ffn_w_down.astype(BF16)
    w_in_b = w_in.astype(BF16)
    w_out_b = w_out.astype(BF16)
    norm_g4 = norm_g.reshape(depth, 3, 1, D_MODEL)
    ws = gmlp_ws.astype(BF16).reshape(depth, N_GMLP_GROUPS * GMLP_CHUNK, GMLP_CHUNK)
    bs = jnp.repeat(jnp.swapaxes(gmlp_bs, 1, 2), HEAD_DIM, axis=2)
    wg, bg = _lru_gate_weights(lru_w_gates, lru_b_gates)
    conv_b3 = conv_b.reshape(depth, 1, D_LRU)
    cs = _channel_dft_matrix()
    tab_lat = _position_dft_tables(seq)
    tab_ctx = _position_dft_tables(ctx_len)
    pos_tab = _pos_table()

    h = x.reshape(batch * seq, D_MODEL)
    hc = ctx.reshape(batch * ctx_len, D_MODEL)
    zero_state = jnp.zeros((batch, 2, D_LRU), F32)

    for l in range(depth):
        last = l == depth - 1
        ffn = functools.partial(_ffn, mod=mod, norm_g=norm_g4, wgu=wgu, wd=wd, layer=l)
        lru = functools.partial(_lru, w_in=w_in_b, conv_w=conv_w, conv_b=conv_b3, wg=wg, bg=bg,
                                lam=lru_lambda, layer=l, batch=batch)
        fourier = functools.partial(_fourier, w_in=w_in_b, cs=cs, layer=l, batch=batch)

        h, n = ffn(h, which=0, pos_tab=pos_tab if l == 0 else None, batch=batch, emit_n=True, **lat)
        hc, n_c = ffn(hc, which=0, emit_n=True, **cx)

        ya_c, state_c = lru(n_c, h0=zero_state, seq=ctx_len)
        ya, _ = lru(n, h0=state_c, seq=seq)
        yc = fourier(n, tables=tab_lat, seq=seq)

        h = ffn(h, which=1, mix=(n, ya, yc, w_in_b, ws, bs, w_out_b),
                final_g=final_norm_g if last else None, **lat)
        if not last:
            yc_c = fourier(n_c, tables=tab_ctx, seq=ctx_len)
            hc = ffn(hc, which=1, mix=(n_c, ya_c, yc_c, w_in_b, ws, bs, w_out_b), **cx)
    return h.reshape(batch, seq, D_MODEL)
```

```python
import functools
import math

import jax
import jax.numpy as jnp
from jax import lax
from jax.experimental import pallas as pl
from jax.experimental.pallas import tpu as pltpu

D_MODEL = 1024
GRID_W = 64
HEAD_DIM = 64
D_FF = ((8 * D_MODEL // 3 + 127) // 128) * 128
D_LRU = D_MODEL // 2
N_LRU_HEADS = D_LRU // HEAD_DIM
D_GMLP = D_MODEL // 4
N_GMLP_GROUPS = D_GMLP // HEAD_DIM
GMLP_CHUNK = 128
D_FNET = D_MODEL // 4
N_FNET_GROUPS = D_FNET // HEAD_DIM
D_IN = 2 * D_LRU + 2 * D_GMLP + D_FNET
CONV_W = 4
LRU_C = 8.0
N_MOD = 9
EPS = 1e-6

LANES = 128
SUBLANES = 8
VMEM_LIMIT_BYTES = 56 * 1024 * 1024

ROW_TILE = 1024
MIX_FFN_TILE = 512
FF_CHUNK = 256
MOD_ROWS = 16
LRU_GROUP = LANES
N_LRU_GROUPS = D_LRU // LRU_GROUP
N_SEG = SUBLANES
LRU_CHUNK = 64
DFT_Q = 256

BF16 = jnp.bfloat16
F32 = jnp.float32
F32_TINY = float(jnp.finfo(jnp.float32).tiny)
LOG2_E = math.log2(math.e)


def _cparams(sem):
    return pltpu.CompilerParams(dimension_semantics=sem, vmem_limit_bytes=VMEM_LIMIT_BYTES)


def _resident(shape, index_map):
    return pl.BlockSpec(shape, index_map, pipeline_mode=pl.Buffered(1))


def _adaln_kernel(c_ref, w_ref, b_ref, o_ref):
    c = c_ref[...]
    s = (c * jax.nn.sigmoid(c)).astype(BF16)
    o_ref[...] = jnp.dot(s, w_ref[...].astype(BF16), preferred_element_type=F32) + b_ref[...]


def _adaln(cvec, w_mod, b_mod):
    depth = w_mod.shape[0]
    return pl.pallas_call(
        _adaln_kernel,
        grid=(depth, N_MOD),
        in_specs=[
            pl.BlockSpec((MOD_ROWS, D_MODEL), lambda l, j: (0, 0)),
            pl.BlockSpec((None, D_MODEL, D_MODEL), lambda l, j: (l, 0, j)),
            pl.BlockSpec((None, 1, D_MODEL), lambda l, j: (l, 0, j)),
        ],
        out_specs=pl.BlockSpec((None, MOD_ROWS, D_MODEL), lambda l, j: (l, 0, j)),
        out_shape=jax.ShapeDtypeStruct((depth, MOD_ROWS, N_MOD * D_MODEL), F32),
        compiler_params=_cparams(("arbitrary", "arbitrary")),
        name="adaln",
    )(cvec, w_mod, b_mod.reshape(depth, 1, N_MOD * D_MODEL))


def _modulated_norm(h, g, shift, scale):
    y = h * lax.rsqrt(jnp.mean(h * h, axis=-1, keepdims=True) + EPS)
    return ((y * g) * (1.0 + scale) + shift).astype(BF16)


def _mod_spec(layer, tiles_per_seq, fixed_row):
    if fixed_row is not None:
        index = lambda i: (layer, fixed_row, 0, 0)
    else:
        index = lambda i: (layer, i // tiles_per_seq, 0, 0)
    return pl.BlockSpec((None, None, N_MOD, D_MODEL), index)


def _row_spec(width, tile=ROW_TILE):
    return pl.BlockSpec((tile, width), lambda i: (i, 0))


def _layer_resident(shape, *lead):
    zeros = (0,) * len(shape)
    return _resident((None,) * len(lead) + shape, lambda i: lead + zeros)


def _lru_coefficients(g, xh, c_half):
    t_r = jnp.tanh(g[:, :LRU_GROUP])
    t_i = jnp.tanh(g[:, LRU_GROUP:])
    m = c_half * t_r + c_half
    a = jnp.exp2(m * (-LOG2_E))
    z = jnp.tanh(m) * (a * a + 1.0)
    root = z * lax.rsqrt(jnp.maximum(z, F32_TINY))
    b = (root * xh) * (t_i + 1.0)
    return a, b


def _lru_kernel(xa_ref, ga_ref, cw_ref, cb_ref, wg_ref, bg_ref, lam_ref, h0_ref,
                ya_ref, st_ref, xin_ref, xc_ref, pf_ref, hf_ref, pb_ref, hb_ref, *, seq):
    seg = seq // N_SEG
    chunk = min(LRU_CHUNK, seg)
    blk = chunk * N_SEG
    n_blk = seg // chunk
    v = N_SEG

    for j in range(N_SEG):
        xin_ref[pl.ds(2 * v + j, seg, stride=N_SEG), :] = xa_ref[j * seg:(j + 1) * seg, :]
    zrow = jnp.zeros((1, LRU_GROUP), F32)
    for k in (0, 1):
        tail = xin_ref[(seg + k) * v:(seg + k + 1) * v, :]
        xin_ref[k * v:(k + 1) * v, :] = jnp.concatenate([zrow, tail[:N_SEG - 1]], axis=0)
    head = xin_ref[2 * v:3 * v, :]
    xin_ref[(seg + 2) * v:(seg + 3) * v, :] = jnp.concatenate([head[1:], zrow], axis=0)

    cw = 0.5 * cw_ref[...]
    cb = 0.5 * cb_ref[...]
    for c in range(n_blk):
        acc = cb
        for k in range(CONV_W):
            acc = acc + xin_ref[c * blk + k * v:c * blk + k * v + blk, :] * cw[k:k + 1, :]
        xc_ref[c * blk:(c + 1) * blk, :] = acc

    c_half = (0.5 * LRU_C) * jax.nn.softplus(-lam_ref[...])
    bg = bg_ref[...]

    def block(c, d, state, p_ref, h_ref):
        rows = slice(c * blk, (c + 1) * blk)
        xh = xc_ref[rows, :]
        cols = slice(2 * d * LRU_GROUP, (2 * d + 2) * LRU_GROUP)
        g = jnp.dot(xh.astype(BF16), wg_ref[:, cols], preferred_element_type=F32) + bg[:, cols]
        a, b = _lru_coefficients(g, xh, c_half[d:d + 1, :])
        p, h = state
        ps, hs = [], []
        steps = range(chunk) if d == 0 else reversed(range(chunk))
        for s in steps:
            a_s = a[s * v:(s + 1) * v, :]
            p = a_s * p
            h = a_s * h + b[s * v:(s + 1) * v, :]
            ps.append(p)
            hs.append(h)
        if d == 1:
            ps, hs = ps[::-1], hs[::-1]
        p_ref[rows, :] = jnp.concatenate(ps, axis=0)
        h_ref[rows, :] = jnp.concatenate(hs, axis=0)
        return p, h

    one = jnp.ones((N_SEG, LRU_GROUP), F32)
    zero = jnp.zeros((N_SEG, LRU_GROUP), F32)
    fwd = bwd = (one, zero)
    for c in range(n_blk):
        fwd = block(c, 0, fwd, pf_ref, hf_ref)
        bwd = block(n_blk - 1 - c, 1, bwd, pb_ref, hb_ref)
    (pf, hf), (pb, hb) = fwd, bwd

    h0 = h0_ref[...]
    c = h0[0:1, :]
    rows = []
    for j in range(N_SEG):
        rows.append(c)
        c = hf[j:j + 1, :] + pf[j:j + 1, :] * c
    carry_f = jnp.concatenate(rows * chunk, axis=0)
    final_f = c
    c = h0[1:2, :]
    rows = []
    for j in reversed(range(N_SEG)):
        rows.append(c)
        c = hb[j:j + 1, :] + pb[j:j + 1, :] * c
    carry_b = jnp.concatenate(rows[::-1] * chunk, axis=0)
    final_b = c
    st_ref[...] = jnp.concatenate([final_f, final_b], axis=0)

    for c in range(n_blk):
        rows = slice(c * blk, (c + 1) * blk)
        xin_ref[rows, :] = ((hf_ref[rows, :] + pf_ref[rows, :] * carry_f)
                            + (hb_ref[rows, :] + pb_ref[rows, :] * carry_b))
    for j in range(N_SEG):
        y = xin_ref[pl.ds(j, seg, stride=N_SEG), :]
        ga = ga_ref[j * seg:(j + 1) * seg, :].astype(F32)
        ya_ref[j * seg:(j + 1) * seg, :] = (y * ga).astype(BF16)


def _lru(xa, ga, conv_w, conv_b, wg, bg, lam, h0, *, layer, batch, seq):
    seg = seq // N_SEG
    blk = lambda: pl.BlockSpec((seq, LRU_GROUP), lambda b, g: (b, g))
    grp = lambda r: pl.BlockSpec((None, r, LRU_GROUP), lambda b, g: (layer, 0, g))
    st = lambda: pl.BlockSpec((None, 2, LRU_GROUP), lambda b, g: (b, 0, g))
    return pl.pallas_call(
        functools.partial(_lru_kernel, seq=seq),
        grid=(batch, N_LRU_GROUPS),
        in_specs=[
            blk(), blk(), grp(CONV_W), grp(1),
            pl.BlockSpec((None, None, LRU_GROUP, 4 * LRU_GROUP), lambda b, g: (layer, g, 0, 0)),
            pl.BlockSpec((None, None, 1, 4 * LRU_GROUP), lambda b, g: (layer, g, 0, 0)),
            grp(2), st(),
        ],
        out_specs=[blk(), st()],
        out_shape=[
            jax.ShapeDtypeStruct((batch * seq, D_LRU), BF16),
            jax.ShapeDtypeStruct((batch, 2, D_LRU), F32),
        ],
        scratch_shapes=[pltpu.VMEM(((seg + CONV_W - 1) * N_SEG, LRU_GROUP), F32)]
        + [pltpu.VMEM((seq, LRU_GROUP), F32) for _ in range(5)],
        compiler_params=_cparams(("arbitrary", "arbitrary")),
        name="lru",
    )(xa, ga, conv_w, conv_b, wg, bg, lam, h0)


def _lru_gate_weights(w_gates, b_gates):
    depth = w_gates.shape[0]
    heads_per_group = LRU_GROUP // HEAD_DIM
    w = w_gates.reshape(depth, 2, 2, N_LRU_GROUPS, heads_per_group, HEAD_DIM, HEAD_DIM)
    eye = jnp.eye(heads_per_group, dtype=w.dtype)
    bd = w[:, :, :, :, :, :, None, :] * eye[None, None, None, None, :, None, :, None]
    bd = bd.reshape(depth, 2, 2, N_LRU_GROUPS, LRU_GROUP, LRU_GROUP)
    wg = jnp.transpose(bd, (0, 3, 4, 1, 2, 5)).reshape(depth, N_LRU_GROUPS, LRU_GROUP, 4 * LRU_GROUP)
    b = 0.5 * b_gates.reshape(depth, 2, 2, N_LRU_GROUPS, LRU_GROUP)
    bg = jnp.transpose(b, (0, 3, 1, 2, 4)).reshape(depth, N_LRU_GROUPS, 1, 4 * LRU_GROUP)
    return wg.astype(BF16), bg


def _mixer_output(h, gate, ya_ref, gu_ref, gv_ref, yc_ref, ws_ref, bs_ref, wo_ref):
    tm = h.shape[0]
    half = D_GMLP // 2
    lane = lax.broadcasted_iota(jnp.int32, (GMLP_CHUNK, half), 1)
    first_group = lane < HEAD_DIM
    yb = []
    for c in range(tm // GMLP_CHUNK):
        rows = slice(c * GMLP_CHUNK, (c + 1) * GMLP_CHUNK)
        cols = []
        for p in range(2):
            m = jnp.dot(ws_ref[p * 2 * GMLP_CHUNK:(p + 1) * 2 * GMLP_CHUNK, :],
                        gv_ref[rows, p * half:(p + 1) * half], preferred_element_type=F32)
            cols.append(jnp.where(first_group, m[:GMLP_CHUNK], m[GMLP_CHUNK:]))
        mixed = jnp.concatenate(cols, axis=1) + bs_ref[...]
        yb.append((gu_ref[rows, :].astype(F32) * mixed).astype(BF16))
    yb = jnp.concatenate(yb, axis=0)
    y = jnp.dot(ya_ref[...], wo_ref[0:D_LRU, :], preferred_element_type=F32)
    y = y + jnp.dot(yb, wo_ref[D_LRU:D_LRU + D_GMLP, :], preferred_element_type=F32)
    y = y + jnp.dot(yc_ref[...], wo_ref[D_LRU + D_GMLP:, :], preferred_element_type=F32)
    return h + gate * y


def _ffn_kernel(*refs, k0, embed, mix, final_norm):
    refs = list(refs)
    take = lambda n: [refs.pop(0) for _ in range(n)]
    (h_ref,) = take(1)
    (tab_ref,) = take(1) if embed else (None,)
    (mod_ref,) = take(1)
    mix_refs = take(7) if mix else None
    g_ref, wgu_ref, wd_ref = take(3)
    (gf_ref,) = take(1) if final_norm else (None,)
    (o_ref,) = take(1)
    assert not refs

    h = h_ref[...]
    if embed:
        tm = h.shape[0]
        n_grid_rows = tm // GRID_W
        tiles_per_seq = pl.num_programs(0) // embed
        r0 = (pl.program_id(0) % tiles_per_seq) * n_grid_rows
        tab = tab_ref[...]
        row_part = jnp.concatenate(
            [jnp.broadcast_to(tab_ref[pl.ds(r0 + a, 1), :], (GRID_W, D_MODEL // 2))
             for a in range(n_grid_rows)], axis=0)
        col_part = jnp.concatenate([tab] * n_grid_rows, axis=0)
        h = h + jnp.concatenate([row_part, col_part], axis=1)
    if mix:
        h = _mixer_output(h, mod_ref[5:6, :], *mix_refs)

    shift = mod_ref[k0:k0 + 1, :]
    scale = mod_ref[k0 + 1:k0 + 2, :]
    gate = mod_ref[k0 + 2:k0 + 3, :]
    n = _modulated_norm(h, g_ref[...], shift, scale)

    act = []
    for j in range(D_FF // FF_CHUNK):
        lo = j * FF_CHUNK
        gt = jnp.dot(n, wgu_ref[:, lo:lo + FF_CHUNK], preferred_element_type=F32)
        up = jnp.dot(n, wgu_ref[:, D_FF + lo:D_FF + lo + FF_CHUNK], preferred_element_type=F32)
        act.append((gt * jax.nn.sigmoid(gt) * up).astype(BF16))
    acc = jnp.dot(jnp.concatenate(act, axis=1), wd_ref[...], preferred_element_type=F32)
    out = h + 0.5 * gate * acc
    if final_norm:
        out = out * lax.rsqrt(jnp.mean(out * out, axis=-1, keepdims=True) + EPS) * gf_ref[...]
    o_ref[...] = out


def _ffn(h, mod, norm_g, wgu, wd, *, layer, which, tiles_per_seq=None, fixed_row=None,
         pos_tab=None, batch=None, mix=None, final_g=None, tile=ROW_TILE):
    rows = h.shape[0]
    rspec = functools.partial(_row_spec, tile=tile)
    in_specs = [rspec(D_MODEL)]
    args = [h]
    if pos_tab is not None:
        in_specs.append(_resident((GRID_W, D_MODEL // 2), lambda i: (0, 0)))
        args.append(pos_tab)
    in_specs.append(_mod_spec(layer, tiles_per_seq, fixed_row))
    args.append(mod)
    if mix is not None:
        ya, gu, gv, yc, ws, bs, w_out = mix
        in_specs += [
            rspec(D_LRU), rspec(D_GMLP), rspec(D_GMLP), rspec(D_FNET),
            _layer_resident((N_GMLP_GROUPS * GMLP_CHUNK, GMLP_CHUNK), layer),
            _layer_resident((GMLP_CHUNK, D_GMLP), layer),
            _layer_resident((D_MODEL, D_MODEL), layer),
        ]
        args += [ya, gu, gv, yc, ws, bs, w_out]
    in_specs += [
        _layer_resident((1, D_MODEL), layer, 2 * which),
        _layer_resident((D_MODEL, 2 * D_FF), layer, which),
        _layer_resident((D_FF, D_MODEL), layer, which),
    ]
    args += [norm_g, wgu, wd]
    if final_g is not None:
        in_specs.append(_resident((1, D_MODEL), lambda i: (0, 0)))
        args.append(final_g.reshape(1, D_MODEL))
    kern = functools.partial(_ffn_kernel, k0=6 * which, embed=batch if pos_tab is not None else 0,
                             mix=mix is not None, final_norm=final_g is not None)
    return pl.pallas_call(
        kern,
        grid=(rows // tile,),
        in_specs=in_specs,
        out_specs=rspec(D_MODEL),
        out_shape=jax.ShapeDtypeStruct((rows, D_MODEL), F32),
        compiler_params=_cparams(("arbitrary",)),
        name="ffn",
    )(*args)


def _mix_in_kernel(h_ref, mod_ref, g_ref, w_ref, xa_ref, ga_ref, gu_ref, gv_ref, f_ref, *scratch,
                   radix):
    h = h_ref[...]
    n = _modulated_norm(h, g_ref[...], mod_ref[3:4, :], mod_ref[4:5, :])

    def proj(lo, width):
        return jnp.dot(n, w_ref[:, lo:lo + width], preferred_element_type=F32)

    f = proj(2 * D_LRU + 2 * D_GMLP, D_FNET)
    if radix == 1:
        f_ref[...] = f.astype(BF16)
    else:
        (fs_ref,) = scratch
        n_slabs = D_FNET // LANES
        for s in range(n_slabs):
            fs_ref[s] = f[:, s * LANES:(s + 1) * LANES]
        per_class = h.shape[0] // radix
        for a in range(radix):
            piece = [fs_ref[s, pl.ds(a, per_class, stride=radix), :] for s in range(n_slabs)]
            f_ref[a] = jnp.concatenate(piece, axis=1).astype(BF16)
    ga_ref[...] = jax.nn.gelu(proj(D_LRU, D_LRU)).astype(BF16)
    gu_ref[...] = jax.nn.gelu(proj(2 * D_LRU, D_GMLP)).astype(BF16)
    gv_ref[...] = jax.nn.gelu(proj(2 * D_LRU + D_GMLP, D_GMLP)).astype(BF16)
    xa_ref[...] = proj(0, D_LRU)


def _mix_in(h, mod, norm_g, w_in, *, layer, radix, tiles_per_seq=None, fixed_row=None):
    rows = h.shape[0]
    out_specs = [_row_spec(D_LRU), _row_spec(D_LRU), _row_spec(D_GMLP), _row_spec(D_GMLP)]
    out_shape = [
        jax.ShapeDtypeStruct((rows, D_LRU), F32),
        jax.ShapeDtypeStruct((rows, D_LRU), BF16),
        jax.ShapeDtypeStruct((rows, D_GMLP), BF16),
        jax.ShapeDtypeStruct((rows, D_GMLP), BF16),
    ]
    scratch = []
    if radix == 1:
        out_specs.append(_row_spec(D_FNET))
        out_shape.append(jax.ShapeDtypeStruct((rows, D_FNET), BF16))
    else:
        per_class = ROW_TILE // radix
        out_specs.append(pl.BlockSpec(
            (None, radix, per_class, D_FNET),
            lambda i: (i // tiles_per_seq, 0, i % tiles_per_seq, 0)))
        out_shape.append(jax.ShapeDtypeStruct(
            (rows // (tiles_per_seq * ROW_TILE), radix, tiles_per_seq * per_class, D_FNET), BF16))
        scratch.append(pltpu.VMEM((D_FNET // LANES, ROW_TILE, LANES), F32))
    return pl.pallas_call(
        functools.partial(_mix_in_kernel, radix=radix),
        grid=(rows // ROW_TILE,),
        in_specs=[
            _row_spec(D_MODEL),
            _mod_spec(layer, tiles_per_seq, fixed_row),
            _layer_resident((1, D_MODEL), layer, 1),
            _layer_resident((D_MODEL, D_IN), layer),
        ],
        out_specs=out_specs,
        out_shape=out_shape,
        scratch_shapes=scratch,
        compiler_params=_cparams(("arbitrary",)),
        name="mix_in",
    )(h, mod, norm_g, w_in)


def _fft(xs):
    n = len(xs)
    if n == 1:
        return xs
    even, odd = _fft(xs[0::2]), _fft(xs[1::2])
    out = [None] * n
    for k in range(n // 2):
        (er, ei), (qr, qi) = even[k], odd[k]
        if k == 0:
            tr, ti = qr, qi
        elif 4 * k == n:
            tr, ti = qi, -qr
        else:
            wr, wi = math.cos(2 * math.pi * k / n), -math.sin(2 * math.pi * k / n)
            tr, ti = wr * qr - wi * qi, wr * qi + wi * qr
        out[k] = (er + tr, ei + ti)
        out[k + n // 2] = (er - tr, ei - ti)
    return out


def _fourier_kernel(x_ref, t_ref, cs_ref, o_ref, u_ref, p_ref, *, radix):
    q = DFT_Q
    n_half = D_FNET // LANES
    for a in range(radix):
        u = jnp.dot(t_ref[a], x_ref[a], preferred_element_type=F32)
        for part in range(2):
            for s in range(n_half):
                dst = p_ref if radix == 1 else u_ref.at[a]
                dst[part * n_half + s] = u[part * q:(part + 1) * q, s * LANES:(s + 1) * LANES]

    if radix > 1:
        def butterflies(i, _):
            s = i // (q // SUBLANES)
            r = pl.multiple_of((i % (q // SUBLANES)) * SUBLANES, SUBLANES)
            xs = [(u_ref[a, s, pl.ds(r, SUBLANES), :], u_ref[a, n_half + s, pl.ds(r, SUBLANES), :])
                  for a in range(radix)]
            for c, (pr, pi) in enumerate(_fft(xs)):
                p_ref[s, pl.ds(c * q + r, SUBLANES), :] = pr
                p_ref[n_half + s, pl.ds(c * q + r, SUBLANES), :] = pi
            return 0

        lax.fori_loop(0, n_half * (q // SUBLANES), butterflies, 0)

    for c in range(radix):
        rows = slice(c * q, (c + 1) * q)
        p = jnp.concatenate([p_ref[s, rows, :] for s in range(2 * n_half)], axis=1).astype(BF16)
        o_ref[rows, :] = jnp.dot(p, cs_ref[...], preferred_element_type=F32).astype(BF16)


def _fourier(f, tables, cs, *, batch, seq):
    radix = seq // DFT_Q
    n_slab = 2 * (D_FNET // LANES)
    return pl.pallas_call(
        functools.partial(_fourier_kernel, radix=radix),
        grid=(batch,),
        in_specs=[
            pl.BlockSpec((None, radix, DFT_Q, D_FNET), lambda b: (b, 0, 0, 0)),
            _resident((radix, 2 * DFT_Q, DFT_Q), lambda b: (0, 0, 0)),
            _resident((2 * D_FNET, D_FNET), lambda b: (0, 0)),
        ],
        out_specs=pl.BlockSpec((seq, D_FNET), lambda b: (b, 0)),
        out_shape=jax.ShapeDtypeStruct((batch * seq, D_FNET), BF16),
        scratch_shapes=[pltpu.VMEM((radix, n_slab, DFT_Q, LANES), F32),
                        pltpu.VMEM((n_slab, seq, LANES), F32)],
        compiler_params=_cparams(("arbitrary",)),
        name="fourier",
    )(f.reshape(batch, radix, DFT_Q, D_FNET), tables, cs)


def _position_dft_tables(seq):
    radix = seq // DFT_Q
    scale = 1.0 / math.sqrt(seq * HEAD_DIM)
    a = jnp.arange(radix, dtype=jnp.int32)[:, None]
    d = jnp.arange(DFT_Q, dtype=jnp.int32)
    ang_a = ((d[None, :] * a) % seq).astype(F32) * (2.0 * math.pi / seq)
    ang_b = ((d[:, None] * d[None, :]) % DFT_Q).astype(F32) * (2.0 * math.pi / DFT_Q)
    ca, sa = (jnp.cos(ang_a) * scale)[:, :, None], (jnp.sin(ang_a) * scale)[:, :, None]
    cb, sb = jnp.cos(ang_b)[None], jnp.sin(ang_b)[None]
    cos = ca * cb - sa * sb
    msin = -(sa * cb + ca * sb)
    return jnp.concatenate([cos, msin], axis=1).astype(BF16)


def _channel_dft_matrix():
    c = jnp.arange(D_FNET, dtype=jnp.int32)
    same_group = (c[:, None] // HEAD_DIM) == (c[None, :] // HEAD_DIM)
    ang = (((c[:, None] % HEAD_DIM) * (c[None, :] % HEAD_DIM)) % HEAD_DIM).astype(F32) * (
        2.0 * math.pi / HEAD_DIM)
    cs = jnp.concatenate([jnp.where(same_group, jnp.cos(ang), 0.0),
                          jnp.where(same_group, jnp.sin(ang), 0.0)], axis=0)
    return cs.astype(BF16)


def _pos_table_kernel(f_ref, o_ref):
    j = lax.broadcasted_iota(jnp.int32, (GRID_W, D_MODEL // 4), 0).astype(F32)
    e = j * f_ref[...]
    o_ref[...] = jnp.concatenate([jnp.sin(e), jnp.cos(e)], axis=1)


def _pos_table():
    q = D_MODEL // 4
    freqs = 1.0 / (10000.0 ** (jnp.arange(q, dtype=F32) / q))
    return pl.pallas_call(
        _pos_table_kernel,
        out_shape=jax.ShapeDtypeStruct((GRID_W, D_MODEL // 2), F32),
        name="pos_table",
    )(freqs.reshape(1, q))


def kernel(x, c, ctx, c_ctx, w_mod, b_mod, norm_g, ffn_w_gu, ffn_w_down, w_in, w_out, conv_w, conv_b,
           lru_w_gates, lru_b_gates, lru_lambda, gmlp_ws, gmlp_bs, final_norm_g):
    batch, seq, _ = x.shape
    ctx_len = ctx.shape[1]
    depth = w_mod.shape[0]
    assert batch < MOD_ROWS and seq % ROW_TILE == 0 and (batch * ctx_len) % ROW_TILE == 0
    assert seq % (N_SEG * LRU_CHUNK) == 0 and ctx_len % (N_SEG * SUBLANES) == 0
    assert LRU_CHUNK % (ctx_len // N_SEG) == 0 or (ctx_len // N_SEG) % LRU_CHUNK == 0
    radix = seq // DFT_Q
    assert seq == radix * DFT_Q and radix & (radix - 1) == 0 and ROW_TILE % radix == 0
    assert ctx_len == DFT_Q and seq % MIX_FFN_TILE == 0
    ctx_row = batch
    lat = dict(tiles_per_seq=seq // ROW_TILE)
    lat_mix = dict(tiles_per_seq=seq // MIX_FFN_TILE, tile=MIX_FFN_TILE)
    cx = dict(fixed_row=ctx_row)

    cvec = jnp.zeros((MOD_ROWS, D_MODEL), F32).at[:batch].set(c).at[ctx_row].set(c_ctx)
    mod = _adaln(cvec, w_mod, b_mod).reshape(depth, MOD_ROWS, N_MOD, D_MODEL)

    wgu = ffn_w_gu.astype(BF16)
    wd = ffn_w_down.astype(BF16)
    w_in_b = w_in.astype(BF16)
    w_out_b = w_out.astype(BF16)
    norm_g4 = norm_g.reshape(depth, 3, 1, D_MODEL)
    ws = gmlp_ws.astype(BF16).reshape(depth, N_GMLP_GROUPS * GMLP_CHUNK, GMLP_CHUNK)
    bs = jnp.repeat(jnp.swapaxes(gmlp_bs, 1, 2), HEAD_DIM, axis=2)
    wg, bg = _lru_gate_weights(lru_w_gates, lru_b_gates)
    conv_b3 = conv_b.reshape(depth, 1, D_LRU)
    cs = _channel_dft_matrix()
    tab_lat = _position_dft_tables(seq)
    tab_ctx = _position_dft_tables(ctx_len)
    pos_tab = _pos_table()

    h = x.reshape(batch * seq, D_MODEL)
    hc = ctx.reshape(batch * ctx_len, D_MODEL)
    zero_state = jnp.zeros((batch, 2, D_LRU), F32)

    for l in range(depth):
        last = l == depth - 1
        ffn = functools.partial(_ffn, mod=mod, norm_g=norm_g4, wgu=wgu, wd=wd, layer=l)
        lru = functools.partial(_lru, conv_w=conv_w, conv_b=conv_b3, wg=wg, bg=bg, lam=lru_lambda,
                                layer=l, batch=batch)
        mix_in = functools.partial(_mix_in, mod=mod, norm_g=norm_g4, w_in=w_in_b, layer=l)

        h = ffn(h, which=0, pos_tab=pos_tab if l == 0 else None, batch=batch, **lat)
        hc = ffn(hc, which=0, **cx)

        xa_c, ga_c, gu_c, gv_c, f_c = mix_in(hc, radix=1, **cx)
        xa, ga, gu, gv, f = mix_in(h, radix=radix, **lat)
        ya_c, state_c = lru(xa_c, ga_c, h0=zero_state, seq=ctx_len)
        ya, _ = lru(xa, ga, h0=state_c, seq=seq)
        yc = _fourier(f, tab_lat, cs, batch=batch, seq=seq)

        h = ffn(h, which=1, mix=(ya, gu, gv, yc, ws, bs, w_out_b),
                final_g=final_norm_g if last else None, **lat_mix)
        if not last:
            yc_c = _fourier(f_c, tab_ctx, cs, batch=batch, seq=ctx_len)
            hc = ffn(hc, which=1, mix=(ya_c, gu_c, gv_c, yc_c, ws, bs, w_out_b), **cx)
    return h.reshape(batch, seq, D_MODEL)
```

```python
import functools
import math

import jax
import jax.numpy as jnp
from jax import lax
from jax.experimental import pallas as pl
from jax.experimental.pallas import tpu as pltpu

D_MODEL = 1024
GRID_W = 64
HEAD_DIM = 64
D_FF = ((8 * D_MODEL // 3 + 127) // 128) * 128
D_LRU = D_MODEL // 2
N_LRU_HEADS = D_LRU // HEAD_DIM
D_GMLP = D_MODEL // 4
N_GMLP_GROUPS = D_GMLP // HEAD_DIM
GMLP_CHUNK = 128
D_FNET = D_MODEL // 4
N_FNET_GROUPS = D_FNET // HEAD_DIM
D_IN = 2 * D_LRU + 2 * D_GMLP + D_FNET
CONV_W = 4
LRU_C = 8.0
N_MOD = 9
EPS = 1e-6

LANES = 128
SUBLANES = 8
VMEM_LIMIT_BYTES = 56 * 1024 * 1024

ROW_TILE = 1024
MIX_FFN_TILE = 512
MIX_IN_TILE = 1024
FF_CHUNK = 256
MOD_ROWS = 16
LRU_GROUP = LANES
N_LRU_GROUPS = D_LRU // LRU_GROUP
N_SEG = SUBLANES
LRU_CHUNK = 64
N_BIAS_ROWS = 2
DFT_Q = 256

BF16 = jnp.bfloat16
F32 = jnp.float32
F32_TINY = float(jnp.finfo(jnp.float32).tiny)
LOG2_E = math.log2(math.e)


def _cparams(sem):
    return pltpu.CompilerParams(dimension_semantics=sem, vmem_limit_bytes=VMEM_LIMIT_BYTES)


def _resident(shape, index_map):
    return pl.BlockSpec(shape, index_map, pipeline_mode=pl.Buffered(1))


def _adaln_kernel(c_ref, w_ref, b_ref, o_ref):
    c = c_ref[...]
    s = (c * jax.nn.sigmoid(c)).astype(BF16)
    o_ref[...] = jnp.dot(s, w_ref[...].astype(BF16), preferred_element_type=F32) + b_ref[...]


def _adaln(cvec, w_mod, b_mod):
    depth = w_mod.shape[0]
    return pl.pallas_call(
        _adaln_kernel,
        grid=(depth, N_MOD),
        in_specs=[
            pl.BlockSpec((MOD_ROWS, D_MODEL), lambda l, j: (0, 0)),
            pl.BlockSpec((None, D_MODEL, D_MODEL), lambda l, j: (l, 0, j)),
            pl.BlockSpec((None, 1, D_MODEL), lambda l, j: (l, 0, j)),
        ],
        out_specs=pl.BlockSpec((None, MOD_ROWS, D_MODEL), lambda l, j: (l, 0, j)),
        out_shape=jax.ShapeDtypeStruct((depth, MOD_ROWS, N_MOD * D_MODEL), F32),
        compiler_params=_cparams(("arbitrary", "arbitrary")),
        name="adaln",
    )(cvec, w_mod, b_mod.reshape(depth, 1, N_MOD * D_MODEL))


def _modulated_norm(h, g, shift, scale):
    y = h * lax.rsqrt(jnp.mean(h * h, axis=-1, keepdims=True) + EPS)
    return ((y * g) * (1.0 + scale) + shift).astype(BF16)


def _mod_spec(layer, tiles_per_seq, fixed_row):
    if fixed_row is not None:
        index = lambda i: (layer, fixed_row, 0, 0)
    else:
        index = lambda i: (layer, i // tiles_per_seq, 0, 0)
    return pl.BlockSpec((None, None, N_MOD, D_MODEL), index)


def _row_spec(width, tile=ROW_TILE):
    return pl.BlockSpec((tile, width), lambda i: (i, 0))


def _layer_resident(shape, *lead):
    zeros = (0,) * len(shape)
    return _resident((None,) * len(lead) + shape, lambda i: lead + zeros)


def _lru_coefficients(g, xh, c_half):
    t_r = jnp.tanh(g[:, :LRU_GROUP])
    t_i = jnp.tanh(g[:, LRU_GROUP:])
    m = c_half * t_r + c_half
    a = jnp.exp2(m * (-LOG2_E))
    z = jnp.tanh(m) * (a * a + 1.0)
    root = z * lax.rsqrt(jnp.maximum(z, F32_TINY))
    b = (root * xh) * (t_i + 1.0)
    return a, b


def _lru_kernel(xa_ref, ga_ref, cw_ref, cb_ref, wg_ref, lam_ref, h0_ref,
                ya_ref, st_ref, xin_ref, xc_ref, pf_ref, hf_ref, pb_ref, hb_ref, *, seq):
    seg = seq // N_SEG
    chunk = min(LRU_CHUNK, seg)
    blk = chunk * N_SEG
    n_blk = seg // chunk
    v = N_SEG

    for j in range(N_SEG):
        xin_ref[pl.ds(2 * v + j, seg, stride=N_SEG), :] = xa_ref[j * seg:(j + 1) * seg, :]
    zrow = jnp.zeros((1, LRU_GROUP), F32)
    for k in (0, 1):
        tail = xin_ref[(seg + k) * v:(seg + k + 1) * v, :]
        xin_ref[k * v:(k + 1) * v, :] = jnp.concatenate([zrow, tail[:N_SEG - 1]], axis=0)
    head = xin_ref[2 * v:3 * v, :]
    xin_ref[(seg + 2) * v:(seg + 3) * v, :] = jnp.concatenate([head[1:], zrow], axis=0)

    cw = 0.5 * cw_ref[...]
    cb = 0.5 * cb_ref[...]
    for c in range(n_blk):
        acc = cb
        for k in range(CONV_W):
            acc = acc + xin_ref[c * blk + k * v:c * blk + k * v + blk, :] * cw[k:k + 1, :]
        xc_ref[c * blk:(c + 1) * blk, :] = acc

    c_half = (0.5 * LRU_C) * jax.nn.softplus(-lam_ref[...])
    lane = lax.broadcasted_iota(jnp.int32, (blk, LRU_GROUP), 1)
    ones = jnp.where(lane < N_BIAS_ROWS, 1.0, 0.0).astype(BF16)

    def block(c, d, state, p_ref, h_ref):
        rows = slice(c * blk, (c + 1) * blk)
        xh = xc_ref[rows, :]
        cols = slice(2 * d * LRU_GROUP, (2 * d + 2) * LRU_GROUP)
        g = jnp.dot(jnp.concatenate([xh.astype(BF16), ones], axis=1), wg_ref[:, cols],
                    preferred_element_type=F32)
        a, b = _lru_coefficients(g, xh, c_half[d:d + 1, :])
        p, h = state
        ps, hs = [], []
        steps = range(chunk) if d == 0 else reversed(range(chunk))
        for s in steps:
            a_s = a[s * v:(s + 1) * v, :]
            p = a_s * p
            h = a_s * h + b[s * v:(s + 1) * v, :]
            ps.append(p)
            hs.append(h)
        if d == 1:
            ps, hs = ps[::-1], hs[::-1]
        p_ref[rows, :] = jnp.concatenate(ps, axis=0)
        h_ref[rows, :] = jnp.concatenate(hs, axis=0)
        return p, h

    one = jnp.ones((N_SEG, LRU_GROUP), F32)
    zero = jnp.zeros((N_SEG, LRU_GROUP), F32)
    fwd = bwd = (one, zero)
    for c in range(n_blk):
        fwd = block(c, 0, fwd, pf_ref, hf_ref)
        bwd = block(n_blk - 1 - c, 1, bwd, pb_ref, hb_ref)
    (pf, hf), (pb, hb) = fwd, bwd

    h0 = h0_ref[...]
    c = h0[0:1, :]
    rows = []
    for j in range(N_SEG):
        rows.append(c)
        c = hf[j:j + 1, :] + pf[j:j + 1, :] * c
    carry_f = jnp.concatenate(rows * chunk, axis=0)
    final_f = c
    c = h0[1:2, :]
    rows = []
    for j in reversed(range(N_SEG)):
        rows.append(c)
        c = hb[j:j + 1, :] + pb[j:j + 1, :] * c
    carry_b = jnp.concatenate(rows[::-1] * chunk, axis=0)
    final_b = c
    st_ref[...] = jnp.concatenate([final_f, final_b], axis=0)

    for c in range(n_blk):
        rows = slice(c * blk, (c + 1) * blk)
        xin_ref[rows, :] = ((hf_ref[rows, :] + pf_ref[rows, :] * carry_f)
                            + (hb_ref[rows, :] + pb_ref[rows, :] * carry_b))
    for j in range(N_SEG):
        y = xin_ref[pl.ds(j, seg, stride=N_SEG), :]
        ga = ga_ref[j * seg:(j + 1) * seg, :].astype(F32)
        ya_ref[j * seg:(j + 1) * seg, :] = (y * ga).astype(BF16)


def _lru(xa, ga, conv_w, conv_b, wg, lam, h0, *, layer, batch, seq):
    seg = seq // N_SEG
    blk = lambda: pl.BlockSpec((seq, LRU_GROUP), lambda b, g: (b, g))
    grp = lambda r: pl.BlockSpec((None, r, LRU_GROUP), lambda b, g: (layer, 0, g))
    st = lambda: pl.BlockSpec((None, 2, LRU_GROUP), lambda b, g: (b, 0, g))
    return pl.pallas_call(
        functools.partial(_lru_kernel, seq=seq),
        grid=(batch, N_LRU_GROUPS),
        in_specs=[
            blk(), blk(), grp(CONV_W), grp(1),
            pl.BlockSpec((None, None, 2 * LRU_GROUP, 4 * LRU_GROUP), lambda b, g: (layer, g, 0, 0)),
            grp(2), st(),
        ],
        out_specs=[blk(), st()],
        out_shape=[
            jax.ShapeDtypeStruct((batch * seq, D_LRU), BF16),
            jax.ShapeDtypeStruct((batch, 2, D_LRU), F32),
        ],
        scratch_shapes=[pltpu.VMEM(((seg + CONV_W - 1) * N_SEG, LRU_GROUP), F32)]
        + [pltpu.VMEM((seq, LRU_GROUP), F32) for _ in range(5)],
        compiler_params=_cparams(("arbitrary", "arbitrary")),
        name="lru",
    )(xa, ga, conv_w, conv_b, wg, lam, h0)


def _lru_gate_weights(w_gates, b_gates):
    depth = w_gates.shape[0]
    heads_per_group = LRU_GROUP // HEAD_DIM
    w = w_gates.astype(BF16).reshape(depth, 2, 2, N_LRU_GROUPS, heads_per_group, HEAD_DIM, HEAD_DIM)
    rows = [jnp.pad(w[:, :, :, :, k], [(0, 0)] * 5 + [(k * HEAD_DIM, LRU_GROUP - (k + 1) * HEAD_DIM)])
            for k in range(heads_per_group)]
    bd = jnp.concatenate(rows, axis=-2)
    wg = jnp.transpose(bd, (0, 3, 4, 1, 2, 5)).reshape(depth, N_LRU_GROUPS, LRU_GROUP, 4 * LRU_GROUP)
    b = 0.5 * b_gates.reshape(depth, 2, 2, N_LRU_GROUPS, LRU_GROUP)
    b = jnp.transpose(b, (0, 3, 1, 2, 4)).reshape(depth, N_LRU_GROUPS, 1, 4 * LRU_GROUP)
    b_hi = b.astype(BF16)
    b_lo = (b - b_hi.astype(F32)).astype(BF16)
    assert N_BIAS_ROWS == 2
    pad = jnp.zeros((depth, N_LRU_GROUPS, LRU_GROUP - N_BIAS_ROWS, 4 * LRU_GROUP), BF16)
    return jnp.concatenate([wg, b_hi, b_lo, pad], axis=2)


def _mixer_output(h, gate, ya_ref, gu_ref, gv_ref, yc_ref, ws_ref, bs_ref, wo_ref):
    tm = h.shape[0]
    half = D_GMLP // 2
    lane = lax.broadcasted_iota(jnp.int32, (GMLP_CHUNK, half), 1)
    first_group = lane < HEAD_DIM
    yb = []
    for c in range(tm // GMLP_CHUNK):
        rows = slice(c * GMLP_CHUNK, (c + 1) * GMLP_CHUNK)
        cols = []
        for p in range(2):
            m = jnp.dot(ws_ref[p * 2 * GMLP_CHUNK:(p + 1) * 2 * GMLP_CHUNK, :],
                        gv_ref[rows, p * half:(p + 1) * half], preferred_element_type=F32)
            cols.append(jnp.where(first_group, m[:GMLP_CHUNK], m[GMLP_CHUNK:]))
        mixed = jnp.concatenate(cols, axis=1) + bs_ref[...]
        yb.append((gu_ref[rows, :].astype(F32) * mixed).astype(BF16))
    yb = jnp.concatenate(yb, axis=0)
    y = jnp.dot(ya_ref[...], wo_ref[0:D_LRU, :], preferred_element_type=F32)
    y = y + jnp.dot(yb, wo_ref[D_LRU:D_LRU + D_GMLP, :], preferred_element_type=F32)
    y = y + jnp.dot(yc_ref[...], wo_ref[D_LRU + D_GMLP:, :], preferred_element_type=F32)
    return h + gate * y


def _ffn_kernel(*refs, k0, embed, mix, final_norm):
    refs = list(refs)
    take = lambda n: [refs.pop(0) for _ in range(n)]
    (h_ref,) = take(1)
    (tab_ref,) = take(1) if embed else (None,)
    (mod_ref,) = take(1)
    mix_refs = take(7) if mix else None
    g_ref, wgu_ref, wd_ref = take(3)
    (gf_ref,) = take(1) if final_norm else (None,)
    (o_ref,) = take(1)
    assert not refs

    h = h_ref[...]
    if embed:
        tm = h.shape[0]
        n_grid_rows = tm // GRID_W
        tiles_per_seq = pl.num_programs(0) // embed
        r0 = (pl.program_id(0) % tiles_per_seq) * n_grid_rows
        tab = tab_ref[...]
        row_part = jnp.concatenate(
            [jnp.broadcast_to(tab_ref[pl.ds(r0 + a, 1), :], (GRID_W, D_MODEL // 2))
             for a in range(n_grid_rows)], axis=0)
        col_part = jnp.concatenate([tab] * n_grid_rows, axis=0)
        h = h + jnp.concatenate([row_part, col_part], axis=1)
    if mix:
        h = _mixer_output(h, mod_ref[5:6, :], *mix_refs)

    shift = mod_ref[k0:k0 + 1, :]
    scale = mod_ref[k0 + 1:k0 + 2, :]
    gate = mod_ref[k0 + 2:k0 + 3, :]
    n = _modulated_norm(h, g_ref[...], shift, scale)

    act = []
    for j in range(D_FF // FF_CHUNK):
        lo = j * FF_CHUNK
        gt = jnp.dot(n, wgu_ref[:, lo:lo + FF_CHUNK], preferred_element_type=F32)
        up = jnp.dot(n, wgu_ref[:, D_FF + lo:D_FF + lo + FF_CHUNK], preferred_element_type=F32)
        act.append((gt * jax.nn.sigmoid(gt) * up).astype(BF16))
    acc = jnp.dot(jnp.concatenate(act, axis=1), wd_ref[...], preferred_element_type=F32)
    out = h + 0.5 * gate * acc
    if final_norm:
        out = out * lax.rsqrt(jnp.mean(out * out, axis=-1, keepdims=True) + EPS) * gf_ref[...]
    o_ref[...] = out


def _ffn(h, mod, norm_g, wgu, wd, *, layer, which, tiles_per_seq=None, fixed_row=None,
         pos_tab=None, batch=None, mix=None, final_g=None, tile=ROW_TILE):
    rows = h.shape[0]
    rspec = functools.partial(_row_spec, tile=tile)
    in_specs = [rspec(D_MODEL)]
    args = [h]
    if pos_tab is not None:
        in_specs.append(_resident((GRID_W, D_MODEL // 2), lambda i: (0, 0)))
        args.append(pos_tab)
    in_specs.append(_mod_spec(layer, tiles_per_seq, fixed_row))
    args.append(mod)
    if mix is not None:
        ya, gu, gv, yc, ws, bs, w_out = mix
        in_specs += [
            rspec(D_LRU), rspec(D_GMLP), rspec(D_GMLP), rspec(D_FNET),
            _layer_resident((N_GMLP_GROUPS * GMLP_CHUNK, GMLP_CHUNK), layer),
            _layer_resident((GMLP_CHUNK, D_GMLP), layer),
            _layer_resident((D_MODEL, D_MODEL), layer),
        ]
        args += [ya, gu, gv, yc, ws, bs, w_out]
    in_specs += [
        _layer_resident((1, D_MODEL), layer, 2 * which),
        _layer_resident((D_MODEL, 2 * D_FF), layer, which),
        _layer_resident((D_FF, D_MODEL), layer, which),
    ]
    args += [norm_g, wgu, wd]
    if final_g is not None:
        in_specs.append(_resident((1, D_MODEL), lambda i: (0, 0)))
        args.append(final_g.reshape(1, D_MODEL))
    kern = functools.partial(_ffn_kernel, k0=6 * which, embed=batch if pos_tab is not None else 0,
                             mix=mix is not None, final_norm=final_g is not None)
    return pl.pallas_call(
        kern,
        grid=(rows // tile,),
        in_specs=in_specs,
        out_specs=rspec(D_MODEL),
        out_shape=jax.ShapeDtypeStruct((rows, D_MODEL), F32),
        compiler_params=_cparams(("arbitrary",)),
        name="ffn",
    )(*args)


def _mix_in_kernel(h_ref, mod_ref, g_ref, w_ref, xa_ref, ga_ref, gu_ref, gv_ref, f_ref, *scratch,
                   radix):
    h = h_ref[...]
    n = _modulated_norm(h, g_ref[...], mod_ref[3:4, :], mod_ref[4:5, :])

    def proj(lo, width):
        return jnp.dot(n, w_ref[:, lo:lo + width], preferred_element_type=F32)

    xa_ref[...] = proj(0, D_LRU)
    ga_ref[...] = jax.nn.gelu(proj(D_LRU, D_LRU)).astype(BF16)
    gu_ref[...] = jax.nn.gelu(proj(2 * D_LRU, D_GMLP)).astype(BF16)
    gv_ref[...] = jax.nn.gelu(proj(2 * D_LRU + D_GMLP, D_GMLP)).astype(BF16)
    f = proj(2 * D_LRU + 2 * D_GMLP, D_FNET)
    if radix == 1:
        f_ref[...] = f.astype(BF16)
        return
    (fs_ref,) = scratch
    n_slabs = D_FNET // LANES
    for s in range(n_slabs):
        fs_ref[s] = f[:, s * LANES:(s + 1) * LANES]
    per_class = h.shape[0] // radix
    for a in range(radix):
        piece = [fs_ref[s, pl.ds(a, per_class, stride=radix), :] for s in range(n_slabs)]
        f_ref[a] = jnp.concatenate(piece, axis=1).astype(BF16)


def _mix_in(h, mod, norm_g, w_in, *, layer, radix, tiles_per_seq=None, fixed_row=None):
    rows = h.shape[0]
    tile = MIX_IN_TILE
    rspec = functools.partial(_row_spec, tile=tile)
    out_specs = [rspec(D_LRU), rspec(D_LRU), rspec(D_GMLP), rspec(D_GMLP)]
    out_shape = [
        jax.ShapeDtypeStruct((rows, D_LRU), F32),
        jax.ShapeDtypeStruct((rows, D_LRU), BF16),
        jax.ShapeDtypeStruct((rows, D_GMLP), BF16),
        jax.ShapeDtypeStruct((rows, D_GMLP), BF16),
    ]
    scratch = []
    if radix == 1:
        out_specs.append(rspec(D_FNET))
        out_shape.append(jax.ShapeDtypeStruct((rows, D_FNET), BF16))
    else:
        per_class = tile // radix
        out_specs.append(pl.BlockSpec(
            (None, radix, per_class, D_FNET),
            lambda i: (i // tiles_per_seq, 0, i % tiles_per_seq, 0)))
        out_shape.append(jax.ShapeDtypeStruct(
            (rows // (tiles_per_seq * tile), radix, tiles_per_seq * per_class, D_FNET), BF16))
        scratch.append(pltpu.VMEM((D_FNET // LANES, tile, LANES), F32))
    return pl.pallas_call(
        functools.partial(_mix_in_kernel, radix=radix),
        grid=(rows // tile,),
        in_specs=[
            rspec(D_MODEL),
            _mod_spec(layer, tiles_per_seq, fixed_row),
            _layer_resident((1, D_MODEL), layer, 1),
            _layer_resident((D_MODEL, D_IN), layer),
        ],
        out_specs=out_specs,
        out_shape=out_shape,
        scratch_shapes=scratch,
        compiler_params=_cparams(("arbitrary",)),
        name="mix_in",
    )(h, mod, norm_g, w_in)


def _fft(xs):
    n = len(xs)
    if n == 1:
        return xs
    even, odd = _fft(xs[0::2]), _fft(xs[1::2])
    out = [None] * n
    for k in range(n // 2):
        (er, ei), (qr, qi) = even[k], odd[k]
        if k == 0:
            tr, ti = qr, qi
        elif 4 * k == n:
            tr, ti = qi, -qr
        else:
            wr, wi = math.cos(2 * math.pi * k / n), -math.sin(2 * math.pi * k / n)
            tr, ti = wr * qr - wi * qi, wr * qi + wi * qr
        out[k] = (er + tr, ei + ti)
        out[k + n // 2] = (er - tr, ei - ti)
    return out


def _fourier_kernel(x_ref, t_ref, cs_ref, o_ref, u_ref, p_ref, *, radix):
    q = DFT_Q
    n_half = D_FNET // LANES
    for a in range(radix):
        u = jnp.dot(t_ref[a], x_ref[a], preferred_element_type=F32)
        for part in range(2):
            for s in range(n_half):
                dst = p_ref if radix == 1 else u_ref.at[a]
                dst[part * n_half + s] = u[part * q:(part + 1) * q, s * LANES:(s + 1) * LANES]

    if radix > 1:
        def butterflies(i, _):
            s = i // (q // SUBLANES)
            r = pl.multiple_of((i % (q // SUBLANES)) * SUBLANES, SUBLANES)
            xs = [(u_ref[a, s, pl.ds(r, SUBLANES), :], u_ref[a, n_half + s, pl.ds(r, SUBLANES), :])
                  for a in range(radix)]
            for c, (pr, pi) in enumerate(_fft(xs)):
                p_ref[s, pl.ds(c * q + r, SUBLANES), :] = pr
                p_ref[n_half + s, pl.ds(c * q + r, SUBLANES), :] = pi
            return 0

        lax.fori_loop(0, n_half * (q // SUBLANES), butterflies, 0, unroll=2)

    for c in range(radix):
        rows = slice(c * q, (c + 1) * q)
        p = jnp.concatenate([p_ref[s, rows, :] for s in range(2 * n_half)], axis=1).astype(BF16)
        o_ref[rows, :] = jnp.dot(p, cs_ref[...], preferred_element_type=F32).astype(BF16)


def _fourier(f, tables, cs, *, batch, seq):
    radix = seq // DFT_Q
    n_slab = 2 * (D_FNET // LANES)
    return pl.pallas_call(
        functools.partial(_fourier_kernel, radix=radix),
        grid=(batch,),
        in_specs=[
            pl.BlockSpec((None, radix, DFT_Q, D_FNET), lambda b: (b, 0, 0, 0)),
            _resident((radix, 2 * DFT_Q, DFT_Q), lambda b: (0, 0, 0)),
            _resident((2 * D_FNET, D_FNET), lambda b: (0, 0)),
        ],
        out_specs=pl.BlockSpec((seq, D_FNET), lambda b: (b, 0)),
        out_shape=jax.ShapeDtypeStruct((batch * seq, D_FNET), BF16),
        scratch_shapes=[pltpu.VMEM((radix, n_slab, DFT_Q, LANES), F32),
                        pltpu.VMEM((n_slab, seq, LANES), F32)],
        compiler_params=_cparams(("arbitrary",)),
        name="fourier",
    )(f.reshape(batch, radix, DFT_Q, D_FNET), tables, cs)


def _position_dft_tables(seq):
    radix = seq // DFT_Q
    scale = 1.0 / math.sqrt(seq * HEAD_DIM)
    a = jnp.arange(radix, dtype=jnp.int32)[:, None]
    d = jnp.arange(DFT_Q, dtype=jnp.int32)
    ang_a = ((d[None, :] * a) % seq).astype(F32) * (2.0 * math.pi / seq)
    ang_b = ((d[:, None] * d[None, :]) % DFT_Q).astype(F32) * (2.0 * math.pi / DFT_Q)
    ca, sa = (jnp.cos(ang_a) * scale)[:, :, None], (jnp.sin(ang_a) * scale)[:, :, None]
    cb, sb = jnp.cos(ang_b)[None], jnp.sin(ang_b)[None]
    cos = ca * cb - sa * sb
    msin = -(sa * cb + ca * sb)
    return jnp.concatenate([cos, msin], axis=1).astype(BF16)


def _channel_dft_matrix():
    c = jnp.arange(D_FNET, dtype=jnp.int32)
    same_group = (c[:, None] // HEAD_DIM) == (c[None, :] // HEAD_DIM)
    ang = (((c[:, None] % HEAD_DIM) * (c[None, :] % HEAD_DIM)) % HEAD_DIM).astype(F32) * (
        2.0 * math.pi / HEAD_DIM)
    cs = jnp.concatenate([jnp.where(same_group, jnp.cos(ang), 0.0),
                          jnp.where(same_group, jnp.sin(ang), 0.0)], axis=0)
    return cs.astype(BF16)


def _pos_table_kernel(f_ref, o_ref):
    j = lax.broadcasted_iota(jnp.int32, (GRID_W, D_MODEL // 4), 0).astype(F32)
    e = j * f_ref[...]
    o_ref[...] = jnp.concatenate([jnp.sin(e), jnp.cos(e)], axis=1)


def _pos_table():
    q = D_MODEL // 4
    freqs = 1.0 / (10000.0 ** (jnp.arange(q, dtype=F32) / q))
    return pl.pallas_call(
        _pos_table_kernel,
        out_shape=jax.ShapeDtypeStruct((GRID_W, D_MODEL // 2), F32),
        name="pos_table",
    )(freqs.reshape(1, q))


def kernel(x, c, ctx, c_ctx, w_mod, b_mod, norm_g, ffn_w_gu, ffn_w_down, w_in, w_out, conv_w, conv_b,
           lru_w_gates, lru_b_gates, lru_lambda, gmlp_ws, gmlp_bs, final_norm_g):
    batch, seq, _ = x.shape
    ctx_len = ctx.shape[1]
    depth = w_mod.shape[0]
    assert batch < MOD_ROWS and seq % ROW_TILE == 0 and (batch * ctx_len) % ROW_TILE == 0
    assert seq % (N_SEG * LRU_CHUNK) == 0 and ctx_len % (N_SEG * SUBLANES) == 0
    assert LRU_CHUNK % (ctx_len // N_SEG) == 0 or (ctx_len // N_SEG) % LRU_CHUNK == 0
    radix = seq // DFT_Q
    assert seq == radix * DFT_Q and radix & (radix - 1) == 0 and ROW_TILE % radix == 0
    assert ctx_len == DFT_Q and seq % MIX_FFN_TILE == 0
    assert seq % MIX_IN_TILE == 0 and (batch * ctx_len) % MIX_IN_TILE == 0 and MIX_IN_TILE % radix == 0
    ctx_row = batch
    lat = dict(tiles_per_seq=seq // ROW_TILE)
    lat_mix = dict(tiles_per_seq=seq // MIX_FFN_TILE, tile=MIX_FFN_TILE)
    cx = dict(fixed_row=ctx_row)

    cvec = jnp.zeros((MOD_ROWS, D_MODEL), F32).at[:batch].set(c).at[ctx_row].set(c_ctx)
    mod = _adaln(cvec, w_mod, b_mod).reshape(depth, MOD_ROWS, N_MOD, D_MODEL)

    wgu = ffn_w_gu.astype(BF16)
    wd = ffn_w_down.astype(BF16)
    w_in_b = w_in.astype(BF16)
    w_out_b = w_out.astype(BF16)
    norm_g4 = norm_g.reshape(depth, 3, 1, D_MODEL)
    ws = gmlp_ws.astype(BF16).reshape(depth, N_GMLP_GROUPS * GMLP_CHUNK, GMLP_CHUNK)
    bs = jnp.repeat(jnp.swapaxes(gmlp_bs, 1, 2), HEAD_DIM, axis=2)
    wg = _lru_gate_weights(lru_w_gates, lru_b_gates)
    conv_b3 = conv_b.reshape(depth, 1, D_LRU)
    cs = _channel_dft_matrix()
    tab_lat = _position_dft_tables(seq)
    tab_ctx = _position_dft_tables(ctx_len)
    pos_tab = _pos_table()

    h = x.reshape(batch * seq, D_MODEL)
    hc = ctx.reshape(batch * ctx_len, D_MODEL)
    zero_state = jnp.zeros((batch, 2, D_LRU), F32)

    for l in range(depth):
        last = l == depth - 1
        ffn = functools.partial(_ffn, mod=mod, norm_g=norm_g4, wgu=wgu, wd=wd, layer=l)
        lru = functools.partial(_lru, conv_w=conv_w, conv_b=conv_b3, wg=wg, lam=lru_lambda,
                                layer=l, batch=batch)
        mix_in = functools.partial(_mix_in, mod=mod, norm_g=norm_g4, w_in=w_in_b, layer=l)

        h = ffn(h, which=0, pos_tab=pos_tab if l == 0 else None, batch=batch, **lat)
        hc = ffn(hc, which=0, **cx)

        xa_c, ga_c, gu_c, gv_c, f_c = mix_in(hc, radix=1, **cx)
        xa, ga, gu, gv, f = mix_in(h, radix=radix, tiles_per_seq=seq // MIX_IN_TILE)
        ya_c, state_c = lru(xa_c, ga_c, h0=zero_state, seq=ctx_len)
        ya, _ = lru(xa, ga, h0=state_c, seq=seq)
        yc = _fourier(f, tab_lat, cs, batch=batch, seq=seq)

        h = ffn(h, which=1, mix=(ya, gu, gv, yc, ws, bs, w_out_b),
                final_g=final_norm_g if last else None, **lat_mix)
        if not last:
            yc_c = _fourier(f_c, tab_ctx, cs, batch=batch, seq=ctx_len)
            hc = ffn(hc, which=1, mix=(ya_c, gu_c, gv_c, yc_c, ws, bs, w_out_b), **cx)
    return h.reshape(batch, seq, D_MODEL)
```

```python
import functools
import math

import jax
import jax.numpy as jnp
from jax import lax
from jax.experimental import pallas as pl
from jax.experimental.pallas import tpu as pltpu

D_MODEL = 1024
GRID_W = 64
HEAD_DIM = 64
D_FF = ((8 * D_MODEL // 3 + 127) // 128) * 128
D_LRU = D_MODEL // 2
N_LRU_HEADS = D_LRU // HEAD_DIM
D_GMLP = D_MODEL // 4
N_GMLP_GROUPS = D_GMLP // HEAD_DIM
GMLP_CHUNK = 128
D_FNET = D_MODEL // 4
N_FNET_GROUPS = D_FNET // HEAD_DIM
D_IN = 2 * D_LRU + 2 * D_GMLP + D_FNET
CONV_W = 4
LRU_C = 8.0
N_MOD = 9
EPS = 1e-6

LANES = 128
SUBLANES = 8
VMEM_LIMIT_BYTES = 56 * 1024 * 1024

ROW_TILE = 1024
MIX_FFN_TILE = 1024
FFN_PART_ROWS = 512
MIX_IN_TILE = 1024
MIX_IN_PARTS = 4
FF_CHUNK = 256
MOD_ROWS = 16
LRU_GROUP = LANES
N_LRU_GROUPS = D_LRU // LRU_GROUP
N_SEG = SUBLANES
LRU_CHUNK = 64
N_BIAS_ROWS = 2
DFT_Q = 256

BF16 = jnp.bfloat16
F32 = jnp.float32
F32_TINY = float(jnp.finfo(jnp.float32).tiny)
LOG2_E = math.log2(math.e)


def _cparams(sem):
    return pltpu.CompilerParams(dimension_semantics=sem, vmem_limit_bytes=VMEM_LIMIT_BYTES)


def _resident(shape, index_map):
    return pl.BlockSpec(shape, index_map, pipeline_mode=pl.Buffered(1))


def _adaln_kernel(c_ref, w_ref, b_ref, o_ref):
    c = c_ref[...]
    s = (c * jax.nn.sigmoid(c)).astype(BF16)
    o_ref[...] = jnp.dot(s, w_ref[...].astype(BF16), preferred_element_type=F32) + b_ref[...]


def _adaln(cvec, w_mod, b_mod):
    depth = w_mod.shape[0]
    return pl.pallas_call(
        _adaln_kernel,
        grid=(depth, N_MOD),
        in_specs=[
            pl.BlockSpec((MOD_ROWS, D_MODEL), lambda l, j: (0, 0)),
            pl.BlockSpec((None, D_MODEL, D_MODEL), lambda l, j: (l, 0, j)),
            pl.BlockSpec((None, 1, D_MODEL), lambda l, j: (l, 0, j)),
        ],
        out_specs=pl.BlockSpec((None, MOD_ROWS, D_MODEL), lambda l, j: (l, 0, j)),
        out_shape=jax.ShapeDtypeStruct((depth, MOD_ROWS, N_MOD * D_MODEL), F32),
        compiler_params=_cparams(("arbitrary", "arbitrary")),
        name="adaln",
    )(cvec, w_mod, b_mod.reshape(depth, 1, N_MOD * D_MODEL))


def _modulated_norm(h, g, shift, scale):
    y = h * lax.rsqrt(jnp.mean(h * h, axis=-1, keepdims=True) + EPS)
    return ((y * g) * (1.0 + scale) + shift).astype(BF16)


def _mod_spec(layer, tiles_per_seq, fixed_row):
    if fixed_row is not None:
        index = lambda i: (layer, fixed_row, 0, 0)
    else:
        index = lambda i: (layer, i // tiles_per_seq, 0, 0)
    return pl.BlockSpec((None, None, N_MOD, D_MODEL), index)


def _row_spec(width, tile=ROW_TILE):
    return pl.BlockSpec((tile, width), lambda i: (i, 0))


def _layer_resident(shape, *lead):
    zeros = (0,) * len(shape)
    return _resident((None,) * len(lead) + shape, lambda i: lead + zeros)


def _lru_coefficients(g, xh, c_half):
    t_r = jnp.tanh(g[:, :LRU_GROUP])
    t_i = jnp.tanh(g[:, LRU_GROUP:])
    m = c_half * t_r + c_half
    a = jnp.exp2(m * (-LOG2_E))
    z = jnp.tanh(m) * (a * a + 1.0)
    root = z * lax.rsqrt(jnp.maximum(z, F32_TINY))
    b = (root * xh) * (t_i + 1.0)
    return a, b


def _lru_kernel(xa_ref, ga_ref, cw_ref, cb_ref, wg_ref, lam_ref, h0_ref,
                ya_ref, st_ref, xin_ref, xc_ref, pf_ref, hf_ref, pb_ref, hb_ref, *, seq):
    seg = seq // N_SEG
    chunk = min(LRU_CHUNK, seg)
    blk = chunk * N_SEG
    n_blk = seg // chunk
    v = N_SEG

    for j in range(N_SEG):
        xin_ref[pl.ds(2 * v + j, seg, stride=N_SEG), :] = xa_ref[j * seg:(j + 1) * seg, :]
    zrow = jnp.zeros((1, LRU_GROUP), F32)
    for k in (0, 1):
        tail = xin_ref[(seg + k) * v:(seg + k + 1) * v, :]
        xin_ref[k * v:(k + 1) * v, :] = jnp.concatenate([zrow, tail[:N_SEG - 1]], axis=0)
    head = xin_ref[2 * v:3 * v, :]
    xin_ref[(seg + 2) * v:(seg + 3) * v, :] = jnp.concatenate([head[1:], zrow], axis=0)

    cw = 0.5 * cw_ref[...]
    cb = 0.5 * cb_ref[...]
    for c in range(n_blk):
        acc = cb
        for k in range(CONV_W):
            acc = acc + xin_ref[c * blk + k * v:c * blk + k * v + blk, :] * cw[k:k + 1, :]
        xc_ref[c * blk:(c + 1) * blk, :] = acc

    c_half = (0.5 * LRU_C) * jax.nn.softplus(-lam_ref[...])
    lane = lax.broadcasted_iota(jnp.int32, (blk, LRU_GROUP), 1)
    ones = jnp.where(lane < N_BIAS_ROWS, 1.0, 0.0).astype(BF16)

    def block(c, d, state, p_ref, h_ref):
        rows = slice(c * blk, (c + 1) * blk)
        xh = xc_ref[rows, :]
        cols = slice(2 * d * LRU_GROUP, (2 * d + 2) * LRU_GROUP)
        g = jnp.dot(jnp.concatenate([xh.astype(BF16), ones], axis=1), wg_ref[:, cols],
                    preferred_element_type=F32)
        a, b = _lru_coefficients(g, xh, c_half[d:d + 1, :])
        p, h = state
        ps, hs = [], []
        steps = range(chunk) if d == 0 else reversed(range(chunk))
        for s in steps:
            a_s = a[s * v:(s + 1) * v, :]
            p = a_s * p
            h = a_s * h + b[s * v:(s + 1) * v, :]
            ps.append(p)
            hs.append(h)
        if d == 1:
            ps, hs = ps[::-1], hs[::-1]
        p_ref[rows, :] = jnp.concatenate(ps, axis=0)
        h_ref[rows, :] = jnp.concatenate(hs, axis=0)
        return p, h

    one = jnp.ones((N_SEG, LRU_GROUP), F32)
    zero = jnp.zeros((N_SEG, LRU_GROUP), F32)
    fwd = bwd = (one, zero)
    for c in range(n_blk):
        fwd = block(c, 0, fwd, pf_ref, hf_ref)
        bwd = block(n_blk - 1 - c, 1, bwd, pb_ref, hb_ref)
    (pf, hf), (pb, hb) = fwd, bwd

    h0 = h0_ref[...]
    c = h0[0:1, :]
    rows = []
    for j in range(N_SEG):
        rows.append(c)
        c = hf[j:j + 1, :] + pf[j:j + 1, :] * c
    carry_f = jnp.concatenate(rows * chunk, axis=0)
    final_f = c
    c = h0[1:2, :]
    rows = []
    for j in reversed(range(N_SEG)):
        rows.append(c)
        c = hb[j:j + 1, :] + pb[j:j + 1, :] * c
    carry_b = jnp.concatenate(rows[::-1] * chunk, axis=0)
    final_b = c
    st_ref[...] = jnp.concatenate([final_f, final_b], axis=0)

    for c in range(n_blk):
        rows = slice(c * blk, (c + 1) * blk)
        xin_ref[rows, :] = ((hf_ref[rows, :] + pf_ref[rows, :] * carry_f)
                            + (hb_ref[rows, :] + pb_ref[rows, :] * carry_b))
    for j in range(N_SEG):
        y = xin_ref[pl.ds(j, seg, stride=N_SEG), :]
        ga = ga_ref[j * seg:(j + 1) * seg, :].astype(F32)
        ya_ref[j * seg:(j + 1) * seg, :] = (y * ga).astype(BF16)


def _lru(xa, ga, conv_w, conv_b, wg, lam, h0, *, layer, batch, seq):
    seg = seq // N_SEG
    blk = lambda: pl.BlockSpec((seq, LRU_GROUP), lambda b, g: (b, g))
    grp = lambda r: pl.BlockSpec((None, r, LRU_GROUP), lambda b, g: (layer, 0, g))
    st = lambda: pl.BlockSpec((None, 2, LRU_GROUP), lambda b, g: (b, 0, g))
    return pl.pallas_call(
        functools.partial(_lru_kernel, seq=seq),
        grid=(batch, N_LRU_GROUPS),
        in_specs=[
            blk(), blk(), grp(CONV_W), grp(1),
            pl.BlockSpec((None, None, 2 * LRU_GROUP, 4 * LRU_GROUP), lambda b, g: (layer, g, 0, 0)),
            grp(2), st(),
        ],
        out_specs=[blk(), st()],
        out_shape=[
            jax.ShapeDtypeStruct((batch * seq, D_LRU), BF16),
            jax.ShapeDtypeStruct((batch, 2, D_LRU), F32),
        ],
        scratch_shapes=[pltpu.VMEM(((seg + CONV_W - 1) * N_SEG, LRU_GROUP), F32)]
        + [pltpu.VMEM((seq, LRU_GROUP), F32) for _ in range(5)],
        compiler_params=_cparams(("arbitrary", "arbitrary")),
        name="lru",
    )(xa, ga, conv_w, conv_b, wg, lam, h0)


def _lru_gate_weights(w_gates, b_gates):
    depth = w_gates.shape[0]
    heads_per_group = LRU_GROUP // HEAD_DIM
    w = w_gates.astype(BF16).reshape(depth, 2, 2, N_LRU_GROUPS, heads_per_group, HEAD_DIM, HEAD_DIM)
    rows = [jnp.pad(w[:, :, :, :, k], [(0, 0)] * 5 + [(k * HEAD_DIM, LRU_GROUP - (k + 1) * HEAD_DIM)])
            for k in range(heads_per_group)]
    bd = jnp.concatenate(rows, axis=-2)
    wg = jnp.transpose(bd, (0, 3, 4, 1, 2, 5)).reshape(depth, N_LRU_GROUPS, LRU_GROUP, 4 * LRU_GROUP)
    b = 0.5 * b_gates.reshape(depth, 2, 2, N_LRU_GROUPS, LRU_GROUP)
    b = jnp.transpose(b, (0, 3, 1, 2, 4)).reshape(depth, N_LRU_GROUPS, 1, 4 * LRU_GROUP)
    b_hi = b.astype(BF16)
    b_lo = (b - b_hi.astype(F32)).astype(BF16)
    assert N_BIAS_ROWS == 2
    pad = jnp.zeros((depth, N_LRU_GROUPS, LRU_GROUP - N_BIAS_ROWS, 4 * LRU_GROUP), BF16)
    return jnp.concatenate([wg, b_hi, b_lo, pad], axis=2)


def _mixer_output(h, gate, row0, ya_ref, gu_ref, gv_ref, yc_ref, ws_ref, bs_ref, wo_ref):
    tm = h.shape[0]
    half = D_GMLP // 2
    lane = lax.broadcasted_iota(jnp.int32, (GMLP_CHUNK, half), 1)
    first_group = lane < HEAD_DIM
    yb = []
    for c in range(tm // GMLP_CHUNK):
        rows = slice(row0 + c * GMLP_CHUNK, row0 + (c + 1) * GMLP_CHUNK)
        cols = []
        for p in range(2):
            m = jnp.dot(ws_ref[p * 2 * GMLP_CHUNK:(p + 1) * 2 * GMLP_CHUNK, :],
                        gv_ref[rows, p * half:(p + 1) * half], preferred_element_type=F32)
            cols.append(jnp.where(first_group, m[:GMLP_CHUNK], m[GMLP_CHUNK:]))
        mixed = jnp.concatenate(cols, axis=1) + bs_ref[...]
        yb.append((gu_ref[rows, :].astype(F32) * mixed).astype(BF16))
    yb = jnp.concatenate(yb, axis=0)
    part = slice(row0, row0 + tm)
    y = jnp.dot(ya_ref[part, :], wo_ref[0:D_LRU, :], preferred_element_type=F32)
    y = y + jnp.dot(yb, wo_ref[D_LRU:D_LRU + D_GMLP, :], preferred_element_type=F32)
    y = y + jnp.dot(yc_ref[part, :], wo_ref[D_LRU + D_GMLP:, :], preferred_element_type=F32)
    return h + gate * y


def _ffn_kernel(*refs, k0, embed, mix, final_norm, parts):
    refs = list(refs)
    take = lambda n: [refs.pop(0) for _ in range(n)]
    (h_ref,) = take(1)
    (tab_ref,) = take(1) if embed else (None,)
    (mod_ref,) = take(1)
    mix_refs = take(7) if mix else None
    g_ref, wgu_ref, wd_ref = take(3)
    (gf_ref,) = take(1) if final_norm else (None,)
    (o_ref,) = take(1)
    assert not refs

    tile = h_ref.shape[0]
    tm = tile // parts
    shift = mod_ref[k0:k0 + 1, :]
    scale = mod_ref[k0 + 1:k0 + 2, :]
    gate = mod_ref[k0 + 2:k0 + 3, :]
    for part in range(parts):
        row0 = part * tm
        h = h_ref[row0:row0 + tm, :]
        if embed:
            n_grid_rows = tm // GRID_W
            tiles_per_seq = pl.num_programs(0) // embed
            r0 = (pl.program_id(0) % tiles_per_seq) * (tile // GRID_W) + part * n_grid_rows
            tab = tab_ref[...]
            row_part = jnp.concatenate(
                [jnp.broadcast_to(tab_ref[pl.ds(r0 + a, 1), :], (GRID_W, D_MODEL // 2))
                 for a in range(n_grid_rows)], axis=0)
            col_part = jnp.concatenate([tab] * n_grid_rows, axis=0)
            h = h + jnp.concatenate([row_part, col_part], axis=1)
        if mix:
            h = _mixer_output(h, mod_ref[5:6, :], row0, *mix_refs)

        n = _modulated_norm(h, g_ref[...], shift, scale)
        act = []
        for j in range(D_FF // FF_CHUNK):
            lo = j * FF_CHUNK
            gt = jnp.dot(n, wgu_ref[:, lo:lo + FF_CHUNK], preferred_element_type=F32)
            up = jnp.dot(n, wgu_ref[:, D_FF + lo:D_FF + lo + FF_CHUNK], preferred_element_type=F32)
            act.append((gt * jax.nn.sigmoid(gt) * up).astype(BF16))
        acc = jnp.dot(jnp.concatenate(act, axis=1), wd_ref[...], preferred_element_type=F32)
        out = h + 0.5 * gate * acc
        if final_norm:
            out = out * lax.rsqrt(jnp.mean(out * out, axis=-1, keepdims=True) + EPS) * gf_ref[...]
        o_ref[row0:row0 + tm, :] = out


def _ffn(h, mod, norm_g, wgu, wd, *, layer, which, tiles_per_seq=None, fixed_row=None,
         pos_tab=None, batch=None, mix=None, final_g=None, tile=ROW_TILE):
    rows = h.shape[0]
    rspec = functools.partial(_row_spec, tile=tile)
    in_specs = [rspec(D_MODEL)]
    args = [h]
    if pos_tab is not None:
        in_specs.append(_resident((GRID_W, D_MODEL // 2), lambda i: (0, 0)))
        args.append(pos_tab)
    in_specs.append(_mod_spec(layer, tiles_per_seq, fixed_row))
    args.append(mod)
    if mix is not None:
        ya, gu, gv, yc, ws, bs, w_out = mix
        in_specs += [
            rspec(D_LRU), rspec(D_GMLP), rspec(D_GMLP), rspec(D_FNET),
            _layer_resident((N_GMLP_GROUPS * GMLP_CHUNK, GMLP_CHUNK), layer),
            _layer_resident((GMLP_CHUNK, D_GMLP), layer),
            _layer_resident((D_MODEL, D_MODEL), layer),
        ]
        args += [ya, gu, gv, yc, ws, bs, w_out]
    in_specs += [
        _layer_resident((1, D_MODEL), layer, 2 * which),
        _layer_resident((D_MODEL, 2 * D_FF), layer, which),
        _layer_resident((D_FF, D_MODEL), layer, which),
    ]
    args += [norm_g, wgu, wd]
    if final_g is not None:
        in_specs.append(_resident((1, D_MODEL), lambda i: (0, 0)))
        args.append(final_g.reshape(1, D_MODEL))
    kern = functools.partial(_ffn_kernel, k0=6 * which, embed=batch if pos_tab is not None else 0,
                             mix=mix is not None, final_norm=final_g is not None,
                             parts=tile // FFN_PART_ROWS)
    return pl.pallas_call(
        kern,
        grid=(rows // tile,),
        in_specs=in_specs,
        out_specs=rspec(D_MODEL),
        out_shape=jax.ShapeDtypeStruct((rows, D_MODEL), F32),
        compiler_params=_cparams(("arbitrary",)),
        name="ffn",
    )(*args)


def _mix_in_kernel(h_ref, mod_ref, g_ref, w_ref, xa_ref, ga_ref, gu_ref, gv_ref, f_ref, *scratch,
                   radix):
    tile = h_ref.shape[0]
    n_slabs = D_FNET // LANES
    for part in range(MIX_IN_PARTS):
        rows = slice(part * tile // MIX_IN_PARTS, (part + 1) * tile // MIX_IN_PARTS)
        n = _modulated_norm(h_ref[rows, :], g_ref[...], mod_ref[3:4, :], mod_ref[4:5, :])

        def proj(lo, width):
            return jnp.dot(n, w_ref[:, lo:lo + width], preferred_element_type=F32)

        xa_ref[rows, :] = proj(0, D_LRU)
        ga_ref[rows, :] = jax.nn.gelu(proj(D_LRU, D_LRU)).astype(BF16)
        gu_ref[rows, :] = jax.nn.gelu(proj(2 * D_LRU, D_GMLP)).astype(BF16)
        gv_ref[rows, :] = jax.nn.gelu(proj(2 * D_LRU + D_GMLP, D_GMLP)).astype(BF16)
        f = proj(2 * D_LRU + 2 * D_GMLP, D_FNET)
        if radix == 1:
            f_ref[rows, :] = f.astype(BF16)
        else:
            for s in range(n_slabs):
                scratch[0][s, rows, :] = f[:, s * LANES:(s + 1) * LANES]
    if radix == 1:
        return
    (fs_ref,) = scratch
    per_class = tile // radix
    for a in range(radix):
        piece = [fs_ref[s, pl.ds(a, per_class, stride=radix), :] for s in range(n_slabs)]
        f_ref[a] = jnp.concatenate(piece, axis=1).astype(BF16)


def _mix_in(h, mod, norm_g, w_in, *, layer, radix, tiles_per_seq=None, fixed_row=None):
    rows = h.shape[0]
    tile = MIX_IN_TILE
    rspec = functools.partial(_row_spec, tile=tile)
    out_specs = [rspec(D_LRU), rspec(D_LRU), rspec(D_GMLP), rspec(D_GMLP)]
    out_shape = [
        jax.ShapeDtypeStruct((rows, D_LRU), F32),
        jax.ShapeDtypeStruct((rows, D_LRU), BF16),
        jax.ShapeDtypeStruct((rows, D_GMLP), BF16),
        jax.ShapeDtypeStruct((rows, D_GMLP), BF16),
    ]
    scratch = []
    if radix == 1:
        out_specs.append(rspec(D_FNET))
        out_shape.append(jax.ShapeDtypeStruct((rows, D_FNET), BF16))
    else:
        per_class = tile // radix
        out_specs.append(pl.BlockSpec(
            (None, radix, per_class, D_FNET),
            lambda i: (i // tiles_per_seq, 0, i % tiles_per_seq, 0)))
        out_shape.append(jax.ShapeDtypeStruct(
            (rows // (tiles_per_seq * tile), radix, tiles_per_seq * per_class, D_FNET), BF16))
        scratch.append(pltpu.VMEM((D_FNET // LANES, tile, LANES), F32))
    return pl.pallas_call(
        functools.partial(_mix_in_kernel, radix=radix),
        grid=(rows // tile,),
        in_specs=[
            rspec(D_MODEL),
            _mod_spec(layer, tiles_per_seq, fixed_row),
            _layer_resident((1, D_MODEL), layer, 1),
            _layer_resident((D_MODEL, D_IN), layer),
        ],
        out_specs=out_specs,
        out_shape=out_shape,
        scratch_shapes=scratch,
        compiler_params=_cparams(("arbitrary",)),
        name="mix_in",
    )(h, mod, norm_g, w_in)


def _fft(xs):
    n = len(xs)
    if n == 1:
        return xs
    even, odd = _fft(xs[0::2]), _fft(xs[1::2])
    out = [None] * n
    for k in range(n // 2):
        (er, ei), (qr, qi) = even[k], odd[k]
        if k == 0:
            tr, ti = qr, qi
        elif 4 * k == n:
            tr, ti = qi, -qr
        else:
            wr, wi = math.cos(2 * math.pi * k / n), -math.sin(2 * math.pi * k / n)
            tr, ti = wr * qr - wi * qi, wr * qi + wi * qr
        out[k] = (er + tr, ei + ti)
        out[k + n // 2] = (er - tr, ei - ti)
    return out


def _fourier_kernel(x_ref, t_ref, cs_ref, o_ref, u_ref, p_ref, *, radix):
    q = DFT_Q
    n_half = D_FNET // LANES
    for a in range(radix):
        u = jnp.dot(t_ref[a], x_ref[a], preferred_element_type=F32)
        for part in range(2):
            for s in range(n_half):
                dst = p_ref if radix == 1 else u_ref.at[a]
                dst[part * n_half + s] = u[part * q:(part + 1) * q, s * LANES:(s + 1) * LANES]

    if radix > 1:
        def butterflies(i, _):
            s = i // (q // SUBLANES)
            r = pl.multiple_of((i % (q // SUBLANES)) * SUBLANES, SUBLANES)
            xs = [(u_ref[a, s, pl.ds(r, SUBLANES), :], u_ref[a, n_half + s, pl.ds(r, SUBLANES), :])
                  for a in range(radix)]
            for c, (pr, pi) in enumerate(_fft(xs)):
                p_ref[s, pl.ds(c * q + r, SUBLANES), :] = pr
                p_ref[n_half + s, pl.ds(c * q + r, SUBLANES), :] = pi
            return 0

        lax.fori_loop(0, n_half * (q // SUBLANES), butterflies, 0, unroll=2)

    for c in range(radix):
        rows = slice(c * q, (c + 1) * q)
        p = jnp.concatenate([p_ref[s, rows, :] for s in range(2 * n_half)], axis=1).astype(BF16)
        o_ref[rows, :] = jnp.dot(p, cs_ref[...], preferred_element_type=F32).astype(BF16)


def _fourier(f, tables, cs, *, batch, seq):
    radix = seq // DFT_Q
    n_slab = 2 * (D_FNET // LANES)
    return pl.pallas_call(
        functools.partial(_fourier_kernel, radix=radix),
        grid=(batch,),
        in_specs=[
            pl.BlockSpec((None, radix, DFT_Q, D_FNET), lambda b: (b, 0, 0, 0)),
            _resident((radix, 2 * DFT_Q, DFT_Q), lambda b: (0, 0, 0)),
            _resident((2 * D_FNET, D_FNET), lambda b: (0, 0)),
        ],
        out_specs=pl.BlockSpec((seq, D_FNET), lambda b: (b, 0)),
        out_shape=jax.ShapeDtypeStruct((batch * seq, D_FNET), BF16),
        scratch_shapes=[pltpu.VMEM((radix, n_slab, DFT_Q, LANES), F32),
                        pltpu.VMEM((n_slab, seq, LANES), F32)],
        compiler_params=_cparams(("arbitrary",)),
        name="fourier",
    )(f.reshape(batch, radix, DFT_Q, D_FNET), tables, cs)


def _position_dft_tables(seq):
    radix = seq // DFT_Q
    scale = 1.0 / math.sqrt(seq * HEAD_DIM)
    a = jnp.arange(radix, dtype=jnp.int32)[:, None]
    d = jnp.arange(DFT_Q, dtype=jnp.int32)
    ang_a = ((d[None, :] * a) % seq).astype(F32) * (2.0 * math.pi / seq)
    ang_b = ((d[:, None] * d[None, :]) % DFT_Q).astype(F32) * (2.0 * math.pi / DFT_Q)
    ca, sa = (jnp.cos(ang_a) * scale)[:, :, None], (jnp.sin(ang_a) * scale)[:, :, None]
    cb, sb = jnp.cos(ang_b)[None], jnp.sin(ang_b)[None]
    cos = ca * cb - sa * sb
    msin = -(sa * cb + ca * sb)
    return jnp.concatenate([cos, msin], axis=1).astype(BF16)


def _channel_dft_matrix():
    c = jnp.arange(D_FNET, dtype=jnp.int32)
    same_group = (c[:, None] // HEAD_DIM) == (c[None, :] // HEAD_DIM)
    ang = (((c[:, None] % HEAD_DIM) * (c[None, :] % HEAD_DIM)) % HEAD_DIM).astype(F32) * (
        2.0 * math.pi / HEAD_DIM)
    cs = jnp.concatenate([jnp.where(same_group, jnp.cos(ang), 0.0),
                          jnp.where(same_group, jnp.sin(ang), 0.0)], axis=0)
    return cs.astype(BF16)


def _pos_table_kernel(f_ref, o_ref):
    j = lax.broadcasted_iota(jnp.int32, (GRID_W, D_MODEL // 4), 0).astype(F32)
    e = j * f_ref[...]
    o_ref[...] = jnp.concatenate([jnp.sin(e), jnp.cos(e)], axis=1)


def _pos_table():
    q = D_MODEL // 4
    freqs = 1.0 / (10000.0 ** (jnp.arange(q, dtype=F32) / q))
    return pl.pallas_call(
        _pos_table_kernel,
        out_shape=jax.ShapeDtypeStruct((GRID_W, D_MODEL // 2), F32),
        name="pos_table",
    )(freqs.reshape(1, q))


def kernel(x, c, ctx, c_ctx, w_mod, b_mod, norm_g, ffn_w_gu, ffn_w_down, w_in, w_out, conv_w, conv_b,
           lru_w_gates, lru_b_gates, lru_lambda, gmlp_ws, gmlp_bs, final_norm_g):
    batch, seq, _ = x.shape
    ctx_len = ctx.shape[1]
    depth = w_mod.shape[0]
    assert batch < MOD_ROWS and seq % ROW_TILE == 0 and (batch * ctx_len) % ROW_TILE == 0
    assert seq % (N_SEG * LRU_CHUNK) == 0 and ctx_len % (N_SEG * SUBLANES) == 0
    assert LRU_CHUNK % (ctx_len // N_SEG) == 0 or (ctx_len // N_SEG) % LRU_CHUNK == 0
    radix = seq // DFT_Q
    assert seq == radix * DFT_Q and radix & (radix - 1) == 0 and ROW_TILE % radix == 0
    assert ctx_len == DFT_Q and seq % MIX_FFN_TILE == 0
    assert seq % MIX_IN_TILE == 0 and (batch * ctx_len) % MIX_IN_TILE == 0 and MIX_IN_TILE % radix == 0
    ctx_row = batch
    lat = dict(tiles_per_seq=seq // ROW_TILE)
    lat_mix = dict(tiles_per_seq=seq // MIX_FFN_TILE, tile=MIX_FFN_TILE)
    cx = dict(fixed_row=ctx_row)

    cvec = jnp.zeros((MOD_ROWS, D_MODEL), F32).at[:batch].set(c).at[ctx_row].set(c_ctx)
    mod = _adaln(cvec, w_mod, b_mod).reshape(depth, MOD_ROWS, N_MOD, D_MODEL)

    wgu = ffn_w_gu.astype(BF16)
    wd = ffn_w_down.astype(BF16)
    w_in_b = w_in.astype(BF16)
    w_out_b = w_out.astype(BF16)
    norm_g4 = norm_g.reshape(depth, 3, 1, D_MODEL)
    ws = gmlp_ws.astype(BF16).reshape(depth, N_GMLP_GROUPS * GMLP_CHUNK, GMLP_CHUNK)
    bs = jnp.repeat(jnp.swapaxes(gmlp_bs, 1, 2), HEAD_DIM, axis=2)
    wg = _lru_gate_weights(lru_w_gates, lru_b_gates)
    conv_b3 = conv_b.reshape(depth, 1, D_LRU)
    cs = _channel_dft_matrix()
    tab_lat = _position_dft_tables(seq)
    tab_ctx = _position_dft_tables(ctx_len)
    pos_tab = _pos_table()

    h = x.reshape(batch * seq, D_MODEL)
    hc = ctx.reshape(batch * ctx_len, D_MODEL)
    zero_state = jnp.zeros((batch, 2, D_LRU), F32)

    for l in range(depth):
        last = l == depth - 1
        ffn = functools.partial(_ffn, mod=mod, norm_g=norm_g4, wgu=wgu, wd=wd, layer=l)
        lru = functools.partial(_lru, conv_w=conv_w, conv_b=conv_b3, wg=wg, lam=lru_lambda,
                                layer=l, batch=batch)
        mix_in = functools.partial(_mix_in, mod=mod, norm_g=norm_g4, w_in=w_in_b, layer=l)

        h = ffn(h, which=0, pos_tab=pos_tab if l == 0 else None, batch=batch, **lat)
        hc = ffn(hc, which=0, **cx)

        xa_c, ga_c, gu_c, gv_c, f_c = mix_in(hc, radix=1, **cx)
        xa, ga, gu, gv, f = mix_in(h, radix=radix, tiles_per_seq=seq // MIX_IN_TILE)
        ya_c, state_c = lru(xa_c, ga_c, h0=zero_state, seq=ctx_len)
        ya, _ = lru(xa, ga, h0=state_c, seq=seq)
        yc = _fourier(f, tab_lat, cs, batch=batch, seq=seq)

        h = ffn(h, which=1, mix=(ya, gu, gv, yc, ws, bs, w_out_b),
                final_g=final_norm_g if last else None, **lat_mix)
        if not last:
            yc_c = _fourier(f_c, tab_ctx, cs, batch=batch, seq=ctx_len)
            hc = ffn(hc, which=1, mix=(ya_c, gu_c, gv_c, yc_c, ws, bs, w_out_b), **cx)
    return h.reshape(batch, seq, D_MODEL)
```

```python
import functools
import math

import jax
import jax.numpy as jnp
from jax import lax
from jax.experimental import pallas as pl
from jax.experimental.pallas import tpu as pltpu

D_MODEL = 1024
GRID_W = 64
HEAD_DIM = 64
D_FF = ((8 * D_MODEL // 3 + 127) // 128) * 128
D_LRU = D_MODEL // 2
N_LRU_HEADS = D_LRU // HEAD_DIM
D_GMLP = D_MODEL // 4
N_GMLP_GROUPS = D_GMLP // HEAD_DIM
GMLP_CHUNK = 128
D_FNET = D_MODEL // 4
N_FNET_GROUPS = D_FNET // HEAD_DIM
D_IN = 2 * D_LRU + 2 * D_GMLP + D_FNET
CONV_W = 4
LRU_C = 8.0
N_MOD = 9
EPS = 1e-6

LANES = 128
SUBLANES = 8
VMEM_LIMIT_BYTES = 56 * 1024 * 1024

ROW_TILE = 1024
MIX_FFN_TILE = 1024
FFN_PART_ROWS = 512
MIX_IN_TILE = 1024
MIX_IN_PARTS = 4
FF_CHUNK = 256
MOD_ROWS = 16
LRU_GROUP = LANES
N_LRU_GROUPS = D_LRU // LRU_GROUP
N_SEG = SUBLANES
LRU_CHUNK = 64
N_BIAS_ROWS = 2
DFT_Q = 256

BF16 = jnp.bfloat16
F32 = jnp.float32
F32_TINY = float(jnp.finfo(jnp.float32).tiny)
LOG2_E = math.log2(math.e)


def _cparams(sem):
    return pltpu.CompilerParams(dimension_semantics=sem, vmem_limit_bytes=VMEM_LIMIT_BYTES)


def _resident(shape, index_map):
    return pl.BlockSpec(shape, index_map, pipeline_mode=pl.Buffered(1))


def _adaln_kernel(c_ref, w_ref, b_ref, o_ref):
    c = c_ref[...]
    s = (c * jax.nn.sigmoid(c)).astype(BF16)
    o_ref[...] = jnp.dot(s, w_ref[...].astype(BF16), preferred_element_type=F32) + b_ref[...]


def _adaln(cvec, w_mod, b_mod):
    depth = w_mod.shape[0]
    return pl.pallas_call(
        _adaln_kernel,
        grid=(depth, N_MOD),
        in_specs=[
            pl.BlockSpec((MOD_ROWS, D_MODEL), lambda l, j: (0, 0)),
            pl.BlockSpec((None, D_MODEL, D_MODEL), lambda l, j: (l, 0, j)),
            pl.BlockSpec((None, 1, D_MODEL), lambda l, j: (l, 0, j)),
        ],
        out_specs=pl.BlockSpec((None, MOD_ROWS, D_MODEL), lambda l, j: (l, 0, j)),
        out_shape=jax.ShapeDtypeStruct((depth, MOD_ROWS, N_MOD * D_MODEL), F32),
        compiler_params=_cparams(("arbitrary", "arbitrary")),
        name="adaln",
    )(cvec, w_mod, b_mod.reshape(depth, 1, N_MOD * D_MODEL))


def _modulated_norm(h, g, shift, scale):
    y = h * lax.rsqrt(jnp.mean(h * h, axis=-1, keepdims=True) + EPS)
    return ((y * g) * (1.0 + scale) + shift).astype(BF16)


def _mod_spec(layer, tiles_per_seq, fixed_row):
    if fixed_row is not None:
        index = lambda i: (layer, fixed_row, 0, 0)
    else:
        index = lambda i: (layer, i // tiles_per_seq, 0, 0)
    return pl.BlockSpec((None, None, N_MOD, D_MODEL), index)


def _row_spec(width, tile=ROW_TILE):
    return pl.BlockSpec((tile, width), lambda i: (i, 0))


def _layer_resident(shape, *lead):
    zeros = (0,) * len(shape)
    return _resident((None,) * len(lead) + shape, lambda i: lead + zeros)


def _lru_coefficients(g, xh, c_half):
    t_r = jnp.tanh(g[:, :LRU_GROUP])
    t_i = jnp.tanh(g[:, LRU_GROUP:])
    m = c_half * t_r + c_half
    a = jnp.exp2(m * (-LOG2_E))
    z = jnp.tanh(m) * (a * a + 1.0)
    root = z * lax.rsqrt(jnp.maximum(z, F32_TINY))
    b = (root * xh) * (t_i + 1.0)
    return a, b


def _lru_kernel(xa_ref, ga_ref, cw_ref, cb_ref, wg_ref, lam_ref, h0_ref,
                ya_ref, st_ref, xin_ref, xc_ref, pf_ref, hf_ref, pb_ref, hb_ref, *, seq):
    seg = seq // N_SEG
    chunk = min(LRU_CHUNK, seg)
    blk = chunk * N_SEG
    n_blk = seg // chunk
    v = N_SEG

    for j in range(N_SEG):
        xin_ref[pl.ds(2 * v + j, seg, stride=N_SEG), :] = xa_ref[j * seg:(j + 1) * seg, :]
    zrow = jnp.zeros((1, LRU_GROUP), F32)
    for k in (0, 1):
        tail = xin_ref[(seg + k) * v:(seg + k + 1) * v, :]
        xin_ref[k * v:(k + 1) * v, :] = jnp.concatenate([zrow, tail[:N_SEG - 1]], axis=0)
    head = xin_ref[2 * v:3 * v, :]
    xin_ref[(seg + 2) * v:(seg + 3) * v, :] = jnp.concatenate([head[1:], zrow], axis=0)

    cw = 0.5 * cw_ref[...]
    cb = 0.5 * cb_ref[...]
    for c in range(n_blk):
        acc = cb
        for k in range(CONV_W):
            acc = acc + xin_ref[c * blk + k * v:c * blk + k * v + blk, :] * cw[k:k + 1, :]
        xc_ref[c * blk:(c + 1) * blk, :] = acc

    c_half = (0.5 * LRU_C) * jax.nn.softplus(-lam_ref[...])
    lane = lax.broadcasted_iota(jnp.int32, (blk, LRU_GROUP), 1)
    ones = jnp.where(lane < N_BIAS_ROWS, 1.0, 0.0).astype(BF16)

    def block(c, d, state, p_ref, h_ref):
        rows = slice(c * blk, (c + 1) * blk)
        xh = xc_ref[rows, :]
        cols = slice(2 * d * LRU_GROUP, (2 * d + 2) * LRU_GROUP)
        g = jnp.dot(jnp.concatenate([xh.astype(BF16), ones], axis=1), wg_ref[:, cols],
                    preferred_element_type=F32)
        a, b = _lru_coefficients(g, xh, c_half[d:d + 1, :])
        p, h = state
        ps, hs = [], []
        steps = range(chunk) if d == 0 else reversed(range(chunk))
        for s in steps:
            a_s = a[s * v:(s + 1) * v, :]
            p = a_s * p
            h = a_s * h + b[s * v:(s + 1) * v, :]
            ps.append(p)
            hs.append(h)
        if d == 1:
            ps, hs = ps[::-1], hs[::-1]
        p_ref[rows, :] = jnp.concatenate(ps, axis=0)
        h_ref[rows, :] = jnp.concatenate(hs, axis=0)
        return p, h

    one = jnp.ones((N_SEG, LRU_GROUP), F32)
    zero = jnp.zeros((N_SEG, LRU_GROUP), F32)
    fwd = bwd = (one, zero)
    for c in range(n_blk):
        fwd = block(c, 0, fwd, pf_ref, hf_ref)
        bwd = block(n_blk - 1 - c, 1, bwd, pb_ref, hb_ref)
    (pf, hf), (pb, hb) = fwd, bwd

    h0 = h0_ref[...]
    c = h0[0:1, :]
    rows = []
    for j in range(N_SEG):
        rows.append(c)
        c = hf[j:j + 1, :] + pf[j:j + 1, :] * c
    carry_f = jnp.concatenate(rows * chunk, axis=0)
    final_f = c
    c = h0[1:2, :]
    rows = []
    for j in reversed(range(N_SEG)):
        rows.append(c)
        c = hb[j:j + 1, :] + pb[j:j + 1, :] * c
    carry_b = jnp.concatenate(rows[::-1] * chunk, axis=0)
    final_b = c
    st_ref[...] = jnp.concatenate([final_f, final_b], axis=0)

    for c in range(n_blk):
        rows = slice(c * blk, (c + 1) * blk)
        xin_ref[rows, :] = ((hf_ref[rows, :] + pf_ref[rows, :] * carry_f)
                            + (hb_ref[rows, :] + pb_ref[rows, :] * carry_b))
    for j in range(N_SEG):
        y = xin_ref[pl.ds(j, seg, stride=N_SEG), :]
        ga = ga_ref[j * seg:(j + 1) * seg, :].astype(F32)
        ya_ref[j * seg:(j + 1) * seg, :] = (y * ga).astype(BF16)


def _lru_kernel_into(xa_ref, ga_ref, cw_ref, cb_ref, wg_ref, lam_ref, h0_ref, into_ref, *rest, seq):
    del into_ref
    _lru_kernel(xa_ref, ga_ref, cw_ref, cb_ref, wg_ref, lam_ref, h0_ref, *rest, seq=seq)


def _lru(xa, ga, conv_w, conv_b, wg, lam, h0, *, layer, batch, seq, first_row=0, into=None):
    seg = seq // N_SEG
    first_block = first_row // seq
    assert first_row == first_block * seq
    blk = lambda: pl.BlockSpec((seq, LRU_GROUP), lambda b, g: (first_block + b, g))
    grp = lambda r: pl.BlockSpec((None, r, LRU_GROUP), lambda b, g: (layer, 0, g))
    st = lambda: pl.BlockSpec((None, 2, LRU_GROUP), lambda b, g: (b, 0, g))
    in_specs = [
        blk(), blk(), grp(CONV_W), grp(1),
        pl.BlockSpec((None, None, 2 * LRU_GROUP, 4 * LRU_GROUP), lambda b, g: (layer, g, 0, 0)),
        grp(2), st(),
    ]
    args = [xa, ga, conv_w, conv_b, wg, lam, h0]
    aliases = {}
    if into is not None:
        aliases = {len(args): 0}
        in_specs.append(pl.BlockSpec(memory_space=pl.ANY))
        args.append(into)
    return pl.pallas_call(
        functools.partial(_lru_kernel if into is None else _lru_kernel_into, seq=seq),
        grid=(batch, N_LRU_GROUPS),
        in_specs=in_specs,
        out_specs=[blk(), st()],
        out_shape=[
            jax.ShapeDtypeStruct((xa.shape[0], D_LRU), BF16),
            jax.ShapeDtypeStruct((batch, 2, D_LRU), F32),
        ],
        scratch_shapes=[pltpu.VMEM(((seg + CONV_W - 1) * N_SEG, LRU_GROUP), F32)]
        + [pltpu.VMEM((seq, LRU_GROUP), F32) for _ in range(5)],
        input_output_aliases=aliases,
        compiler_params=_cparams(("arbitrary", "arbitrary")),
        name="lru",
    )(*args)


def _lru_gate_weights(w_gates, b_gates):
    depth = w_gates.shape[0]
    heads_per_group = LRU_GROUP // HEAD_DIM
    w = w_gates.astype(BF16).reshape(depth, 2, 2, N_LRU_GROUPS, heads_per_group, HEAD_DIM, HEAD_DIM)
    rows = [jnp.pad(w[:, :, :, :, k], [(0, 0)] * 5 + [(k * HEAD_DIM, LRU_GROUP - (k + 1) * HEAD_DIM)])
            for k in range(heads_per_group)]
    bd = jnp.concatenate(rows, axis=-2)
    wg = jnp.transpose(bd, (0, 3, 4, 1, 2, 5)).reshape(depth, N_LRU_GROUPS, LRU_GROUP, 4 * LRU_GROUP)
    b = 0.5 * b_gates.reshape(depth, 2, 2, N_LRU_GROUPS, LRU_GROUP)
    b = jnp.transpose(b, (0, 3, 1, 2, 4)).reshape(depth, N_LRU_GROUPS, 1, 4 * LRU_GROUP)
    b_hi = b.astype(BF16)
    b_lo = (b - b_hi.astype(F32)).astype(BF16)
    assert N_BIAS_ROWS == 2
    pad = jnp.zeros((depth, N_LRU_GROUPS, LRU_GROUP - N_BIAS_ROWS, 4 * LRU_GROUP), BF16)
    return jnp.concatenate([wg, b_hi, b_lo, pad], axis=2)


def _mixer_output(h, gate, row0, ya_ref, gu_ref, gv_ref, yc_ref, ws_ref, bs_ref, wo_ref):
    tm = h.shape[0]
    half = D_GMLP // 2
    lane = lax.broadcasted_iota(jnp.int32, (GMLP_CHUNK, half), 1)
    first_group = lane < HEAD_DIM
    yb = []
    for c in range(tm // GMLP_CHUNK):
        rows = slice(row0 + c * GMLP_CHUNK, row0 + (c + 1) * GMLP_CHUNK)
        cols = []
        for p in range(2):
            m = jnp.dot(ws_ref[p * 2 * GMLP_CHUNK:(p + 1) * 2 * GMLP_CHUNK, :],
                        gv_ref[rows, p * half:(p + 1) * half], preferred_element_type=F32)
            cols.append(jnp.where(first_group, m[:GMLP_CHUNK], m[GMLP_CHUNK:]))
        mixed = jnp.concatenate(cols, axis=1) + bs_ref[...]
        yb.append((gu_ref[rows, :].astype(F32) * mixed).astype(BF16))
    yb = jnp.concatenate(yb, axis=0)
    part = slice(row0, row0 + tm)
    y = jnp.dot(ya_ref[part, :], wo_ref[0:D_LRU, :], preferred_element_type=F32)
    y = y + jnp.dot(yb, wo_ref[D_LRU:D_LRU + D_GMLP, :], preferred_element_type=F32)
    y = y + jnp.dot(yc_ref[part, :], wo_ref[D_LRU + D_GMLP:, :], preferred_element_type=F32)
    return h + gate * y


def _ffn_kernel(*refs, k0, embed, mix, final_norm, parts):
    refs = list(refs)
    take = lambda n: [refs.pop(0) for _ in range(n)]
    (h_ref,) = take(1)
    ctx_ref, tab_ref = take(2) if embed else (None, None)
    (mod_ref,) = take(1)
    mix_refs = take(7) if mix else None
    g_ref, wgu_ref, wd_ref = take(3)
    (gf_ref,) = take(1) if final_norm else (None,)
    (o_ref,) = take(1)
    assert not refs

    tile = h_ref.shape[0]
    tm = tile // parts
    shift = mod_ref[k0:k0 + 1, :]
    scale = mod_ref[k0 + 1:k0 + 2, :]
    gate = mod_ref[k0 + 2:k0 + 3, :]
    for part in range(parts):
        row0 = part * tm
        h = h_ref[row0:row0 + tm, :]
        if embed:
            tiles_per_seq, n_latent_tiles = embed
            n_grid_rows = tm // GRID_W
            r0 = (pl.program_id(0) % tiles_per_seq) * (tile // GRID_W) + part * n_grid_rows
            tab = tab_ref[...]
            row_part = jnp.concatenate(
                [jnp.broadcast_to(tab_ref[pl.ds(r0 + a, 1), :], (GRID_W, D_MODEL // 2))
                 for a in range(n_grid_rows)], axis=0)
            col_part = jnp.concatenate([tab] * n_grid_rows, axis=0)
            h = jnp.where(pl.program_id(0) < n_latent_tiles,
                          h + jnp.concatenate([row_part, col_part], axis=1),
                          ctx_ref[row0:row0 + tm, :])
        if mix:
            h = _mixer_output(h, mod_ref[5:6, :], row0, *mix_refs)

        n = _modulated_norm(h, g_ref[...], shift, scale)
        act = []
        for j in range(D_FF // FF_CHUNK):
            lo = j * FF_CHUNK
            gt = jnp.dot(n, wgu_ref[:, lo:lo + FF_CHUNK], preferred_element_type=F32)
            up = jnp.dot(n, wgu_ref[:, D_FF + lo:D_FF + lo + FF_CHUNK], preferred_element_type=F32)
            act.append((gt * jax.nn.sigmoid(gt) * up).astype(BF16))
        acc = jnp.dot(jnp.concatenate(act, axis=1), wd_ref[...], preferred_element_type=F32)
        out = h + 0.5 * gate * acc
        if final_norm:
            out = out * lax.rsqrt(jnp.mean(out * out, axis=-1, keepdims=True) + EPS) * gf_ref[...]
        o_ref[row0:row0 + tm, :] = out


def _ffn(h, mod, norm_g, wgu, wd, *, layer, which, tiles_per_seq, embed=None, mix=None,
         final_g=None, n_tiles=None, tile=ROW_TILE):
    rows = h.shape[0]
    rspec = functools.partial(_row_spec, tile=tile)
    in_specs = [rspec(D_MODEL)]
    args = [h]
    embed_static = 0
    if embed is not None:
        ctx, pos_tab = embed
        n_latent_tiles = rows // tile
        embed_static = (tiles_per_seq, n_latent_tiles)
        rows += ctx.shape[0]
        in_specs = [
            pl.BlockSpec((tile, D_MODEL), lambda i: (jnp.minimum(i, n_latent_tiles - 1), 0)),
            pl.BlockSpec((tile, D_MODEL), lambda i: (jnp.maximum(i - n_latent_tiles, 0), 0)),
            _resident((GRID_W, D_MODEL // 2), lambda i: (0, 0)),
        ]
        args += [ctx, pos_tab]
    n_tiles = rows // tile if n_tiles is None else n_tiles
    in_specs.append(_mod_spec(layer, tiles_per_seq, None))
    args.append(mod)
    if mix is not None:
        ya, gu, gv, yc, ws, bs, w_out = mix
        in_specs += [
            rspec(D_LRU), rspec(D_GMLP), rspec(D_GMLP), rspec(D_FNET),
            _layer_resident((N_GMLP_GROUPS * GMLP_CHUNK, GMLP_CHUNK), layer),
            _layer_resident((GMLP_CHUNK, D_GMLP), layer),
            _layer_resident((D_MODEL, D_MODEL), layer),
        ]
        args += [ya, gu, gv, yc, ws, bs, w_out]
    in_specs += [
        _layer_resident((1, D_MODEL), layer, 2 * which),
        _layer_resident((D_MODEL, 2 * D_FF), layer, which),
        _layer_resident((D_FF, D_MODEL), layer, which),
    ]
    args += [norm_g, wgu, wd]
    if final_g is not None:
        in_specs.append(_resident((1, D_MODEL), lambda i: (0, 0)))
        args.append(final_g.reshape(1, D_MODEL))
    kern = functools.partial(_ffn_kernel, k0=6 * which, embed=embed_static,
                             mix=mix is not None, final_norm=final_g is not None,
                             parts=tile // FFN_PART_ROWS)
    return pl.pallas_call(
        kern,
        grid=(n_tiles,),
        in_specs=in_specs,
        out_specs=rspec(D_MODEL),
        out_shape=jax.ShapeDtypeStruct((n_tiles * tile, D_MODEL), F32),
        compiler_params=_cparams(("arbitrary",)),
        name="ffn",
    )(*args)


def _mix_in_kernel(h_ref, mod_ref, g_ref, w_ref, xa_ref, ga_ref, gu_ref, gv_ref, f_ref, fc_ref,
                   fs_ref, *, radix, n_latent_tiles):
    tile = h_ref.shape[0]
    n_slabs = D_FNET // LANES
    for part in range(MIX_IN_PARTS):
        rows = slice(part * tile // MIX_IN_PARTS, (part + 1) * tile // MIX_IN_PARTS)
        n = _modulated_norm(h_ref[rows, :], g_ref[...], mod_ref[3:4, :], mod_ref[4:5, :])

        def proj(lo, width):
            return jnp.dot(n, w_ref[:, lo:lo + width], preferred_element_type=F32)

        xa_ref[rows, :] = proj(0, D_LRU)
        ga_ref[rows, :] = jax.nn.gelu(proj(D_LRU, D_LRU)).astype(BF16)
        gu_ref[rows, :] = jax.nn.gelu(proj(2 * D_LRU, D_GMLP)).astype(BF16)
        gv_ref[rows, :] = jax.nn.gelu(proj(2 * D_LRU + D_GMLP, D_GMLP)).astype(BF16)
        f = proj(2 * D_LRU + 2 * D_GMLP, D_FNET)
        for s in range(n_slabs):
            fs_ref[s, rows, :] = f[:, s * LANES:(s + 1) * LANES]

    is_latent = pl.program_id(0) < n_latent_tiles

    @pl.when(is_latent)
    def _():
        per_class = tile // radix
        for a in range(radix):
            piece = [fs_ref[s, pl.ds(a, per_class, stride=radix), :] for s in range(n_slabs)]
            f_ref[a] = jnp.concatenate(piece, axis=1).astype(BF16)

    @pl.when(jnp.logical_not(is_latent))
    def _():
        fc_ref[...] = jnp.concatenate([fs_ref[s] for s in range(n_slabs)], axis=1).astype(BF16)


def _mix_in(h, mod, norm_g, w_in, *, layer, radix, tiles_per_seq, n_latent_tiles):
    rows = h.shape[0]
    tile = MIX_IN_TILE
    n_tiles = rows // tile
    rspec = functools.partial(_row_spec, tile=tile)
    per_class = tile // radix
    latent = lambda i: jnp.minimum(i, n_latent_tiles - 1)
    return pl.pallas_call(
        functools.partial(_mix_in_kernel, radix=radix, n_latent_tiles=n_latent_tiles),
        grid=(n_tiles,),
        in_specs=[
            rspec(D_MODEL),
            _mod_spec(layer, tiles_per_seq, None),
            _layer_resident((1, D_MODEL), layer, 1),
            _layer_resident((D_MODEL, D_IN), layer),
        ],
        out_specs=[
            rspec(D_LRU), rspec(D_LRU), rspec(D_GMLP), rspec(D_GMLP),
            pl.BlockSpec((None, radix, per_class, D_FNET),
                         lambda i: (latent(i) // tiles_per_seq, 0, latent(i) % tiles_per_seq, 0)),
            pl.BlockSpec((tile, D_FNET), lambda i: (jnp.maximum(i - n_latent_tiles, 0), 0)),
        ],
        out_shape=[
            jax.ShapeDtypeStruct((rows, D_LRU), F32),
            jax.ShapeDtypeStruct((rows, D_LRU), BF16),
            jax.ShapeDtypeStruct((rows, D_GMLP), BF16),
            jax.ShapeDtypeStruct((rows, D_GMLP), BF16),
            jax.ShapeDtypeStruct((n_latent_tiles // tiles_per_seq, radix, tiles_per_seq * per_class,
                                  D_FNET), BF16),
            jax.ShapeDtypeStruct(((n_tiles - n_latent_tiles) * tile, D_FNET), BF16),
        ],
        scratch_shapes=[pltpu.VMEM((D_FNET // LANES, tile, LANES), F32)],
        compiler_params=_cparams(("arbitrary",)),
        name="mix_in",
    )(h, mod, norm_g, w_in)


def _fft(xs):
    n = len(xs)
    if n == 1:
        return xs
    even, odd = _fft(xs[0::2]), _fft(xs[1::2])
    out = [None] * n
    for k in range(n // 2):
        (er, ei), (qr, qi) = even[k], odd[k]
        if k == 0:
            tr, ti = qr, qi
        elif 4 * k == n:
            tr, ti = qi, -qr
        else:
            wr, wi = math.cos(2 * math.pi * k / n), -math.sin(2 * math.pi * k / n)
            tr, ti = wr * qr - wi * qi, wr * qi + wi * qr
        out[k] = (er + tr, ei + ti)
        out[k + n // 2] = (er - tr, ei - ti)
    return out


def _fourier_kernel(x_ref, t_ref, cs_ref, o_ref, u_ref, p_ref, *, radix):
    q = DFT_Q
    n_half = D_FNET // LANES
    for a in range(radix):
        u = jnp.dot(t_ref[a], x_ref[a], preferred_element_type=F32)
        for part in range(2):
            for s in range(n_half):
                dst = p_ref if radix == 1 else u_ref.at[a]
                dst[part * n_half + s] = u[part * q:(part + 1) * q, s * LANES:(s + 1) * LANES]

    if radix > 1:
        def butterflies(i, _):
            s = i // (q // SUBLANES)
            r = pl.multiple_of((i % (q // SUBLANES)) * SUBLANES, SUBLANES)
            xs = [(u_ref[a, s, pl.ds(r, SUBLANES), :], u_ref[a, n_half + s, pl.ds(r, SUBLANES), :])
                  for a in range(radix)]
            for c, (pr, pi) in enumerate(_fft(xs)):
                p_ref[s, pl.ds(c * q + r, SUBLANES), :] = pr
                p_ref[n_half + s, pl.ds(c * q + r, SUBLANES), :] = pi
            return 0

        lax.fori_loop(0, n_half * (q // SUBLANES), butterflies, 0, unroll=2)

    for c in range(radix):
        rows = slice(c * q, (c + 1) * q)
        p = jnp.concatenate([p_ref[s, rows, :] for s in range(2 * n_half)], axis=1).astype(BF16)
        o_ref[rows, :] = jnp.dot(p, cs_ref[...], preferred_element_type=F32).astype(BF16)


def _fourier_kernel_into(x_ref, t_ref, cs_ref, into_ref, *rest, radix):
    del into_ref
    _fourier_kernel(x_ref, t_ref, cs_ref, *rest, radix=radix)


def _fourier(f, tables, cs, *, batch, seq, out_rows=None, first_row=0, into=None):
    radix = seq // DFT_Q
    n_slab = 2 * (D_FNET // LANES)
    first_block = first_row // seq
    assert first_row == first_block * seq
    in_specs = [
        pl.BlockSpec((None, radix, DFT_Q, D_FNET), lambda b: (b, 0, 0, 0)),
        _resident((radix, 2 * DFT_Q, DFT_Q), lambda b: (0, 0, 0)),
        _resident((2 * D_FNET, D_FNET), lambda b: (0, 0)),
    ]
    args = [f.reshape(batch, radix, DFT_Q, D_FNET), tables, cs]
    aliases = {}
    if into is not None:
        aliases = {len(args): 0}
        in_specs.append(pl.BlockSpec(memory_space=pl.ANY))
        args.append(into)
        out_rows = into.shape[0]
    return pl.pallas_call(
        functools.partial(_fourier_kernel if into is None else _fourier_kernel_into, radix=radix),
        grid=(batch,),
        in_specs=in_specs,
        out_specs=pl.BlockSpec((seq, D_FNET), lambda b: (first_block + b, 0)),
        out_shape=jax.ShapeDtypeStruct((batch * seq if out_rows is None else out_rows, D_FNET), BF16),
        scratch_shapes=[pltpu.VMEM((radix, n_slab, DFT_Q, LANES), F32),
                        pltpu.VMEM((n_slab, seq, LANES), F32)],
        input_output_aliases=aliases,
        compiler_params=_cparams(("arbitrary",)),
        name="fourier",
    )(*args)


def _position_dft_tables(seq):
    radix = seq // DFT_Q
    scale = 1.0 / math.sqrt(seq * HEAD_DIM)
    a = jnp.arange(radix, dtype=jnp.int32)[:, None]
    d = jnp.arange(DFT_Q, dtype=jnp.int32)
    ang_a = ((d[None, :] * a) % seq).astype(F32) * (2.0 * math.pi / seq)
    ang_b = ((d[:, None] * d[None, :]) % DFT_Q).astype(F32) * (2.0 * math.pi / DFT_Q)
    ca, sa = (jnp.cos(ang_a) * scale)[:, :, None], (jnp.sin(ang_a) * scale)[:, :, None]
    cb, sb = jnp.cos(ang_b)[None], jnp.sin(ang_b)[None]
    cos = ca * cb - sa * sb
    msin = -(sa * cb + ca * sb)
    return jnp.concatenate([cos, msin], axis=1).astype(BF16)


def _channel_dft_matrix():
    c = jnp.arange(D_FNET, dtype=jnp.int32)
    same_group = (c[:, None] // HEAD_DIM) == (c[None, :] // HEAD_DIM)
    ang = (((c[:, None] % HEAD_DIM) * (c[None, :] % HEAD_DIM)) % HEAD_DIM).astype(F32) * (
        2.0 * math.pi / HEAD_DIM)
    cs = jnp.concatenate([jnp.where(same_group, jnp.cos(ang), 0.0),
                          jnp.where(same_group, jnp.sin(ang), 0.0)], axis=0)
    return cs.astype(BF16)


def _pos_table_kernel(f_ref, o_ref):
    j = lax.broadcasted_iota(jnp.int32, (GRID_W, D_MODEL // 4), 0).astype(F32)
    e = j * f_ref[...]
    o_ref[...] = jnp.concatenate([jnp.sin(e), jnp.cos(e)], axis=1)


def _pos_table():
    q = D_MODEL // 4
    freqs = 1.0 / (10000.0 ** (jnp.arange(q, dtype=F32) / q))
    return pl.pallas_call(
        _pos_table_kernel,
        out_shape=jax.ShapeDtypeStruct((GRID_W, D_MODEL // 2), F32),
        name="pos_table",
    )(freqs.reshape(1, q))


def kernel(x, c, ctx, c_ctx, w_mod, b_mod, norm_g, ffn_w_gu, ffn_w_down, w_in, w_out, conv_w, conv_b,
           lru_w_gates, lru_b_gates, lru_lambda, gmlp_ws, gmlp_bs, final_norm_g):
    batch, seq, _ = x.shape
    ctx_len = ctx.shape[1]
    depth = w_mod.shape[0]
    assert batch < MOD_ROWS and seq % ROW_TILE == 0 and (batch * ctx_len) % ROW_TILE == 0
    assert seq % (N_SEG * LRU_CHUNK) == 0 and ctx_len % (N_SEG * SUBLANES) == 0
    assert LRU_CHUNK % (ctx_len // N_SEG) == 0 or (ctx_len // N_SEG) % LRU_CHUNK == 0
    radix = seq // DFT_Q
    assert seq == radix * DFT_Q and radix & (radix - 1) == 0 and ROW_TILE % radix == 0
    assert ctx_len == DFT_Q and MIX_FFN_TILE == ROW_TILE
    assert seq % MIX_IN_TILE == 0 and (batch * ctx_len) % MIX_IN_TILE == 0 and MIX_IN_TILE % radix == 0
    ctx_row = batch
    latent_rows = batch * seq
    tiles = dict(tiles_per_seq=seq // ROW_TILE)

    cvec = jnp.zeros((MOD_ROWS, D_MODEL), F32).at[:batch].set(c).at[ctx_row].set(c_ctx)
    mod = _adaln(cvec, w_mod, b_mod).reshape(depth, MOD_ROWS, N_MOD, D_MODEL)

    wgu = ffn_w_gu.astype(BF16)
    wd = ffn_w_down.astype(BF16)
    w_in_b = w_in.astype(BF16)
    w_out_b = w_out.astype(BF16)
    norm_g4 = norm_g.reshape(depth, 3, 1, D_MODEL)
    ws = gmlp_ws.astype(BF16).reshape(depth, N_GMLP_GROUPS * GMLP_CHUNK, GMLP_CHUNK)
    bs = jnp.repeat(jnp.swapaxes(gmlp_bs, 1, 2), HEAD_DIM, axis=2)
    wg = _lru_gate_weights(lru_w_gates, lru_b_gates)
    conv_b3 = conv_b.reshape(depth, 1, D_LRU)
    cs = _channel_dft_matrix()
    tab_lat = _position_dft_tables(seq)
    tab_ctx = _position_dft_tables(ctx_len)
    pos_tab = _pos_table()

    h = x.reshape(latent_rows, D_MODEL)
    zero_state = jnp.zeros((batch, 2, D_LRU), F32)

    for l in range(depth):
        last = l == depth - 1
        ffn = functools.partial(_ffn, mod=mod, norm_g=norm_g4, wgu=wgu, wd=wd, layer=l, **tiles)
        lru = functools.partial(_lru, conv_w=conv_w, conv_b=conv_b3, wg=wg, lam=lru_lambda,
                                layer=l, batch=batch)

        h = ffn(h, which=0,
                embed=(ctx.reshape(batch * ctx_len, D_MODEL), pos_tab) if l == 0 else None)
        xa, ga, gu, gv, f, f_c = _mix_in(h, mod, norm_g4, w_in_b, layer=l, radix=radix,
                                         tiles_per_seq=seq // MIX_IN_TILE,
                                         n_latent_tiles=latent_rows // MIX_IN_TILE)
        ya, state_c = lru(xa, ga, h0=zero_state, seq=ctx_len, first_row=latent_rows)
        ya, _ = lru(xa, ga, h0=state_c, seq=seq, into=ya)
        if last:
            yc = _fourier(f, tab_lat, cs, batch=batch, seq=seq)
        else:
            yc = _fourier(f_c, tab_ctx, cs, batch=batch, seq=ctx_len, out_rows=h.shape[0],
                          first_row=latent_rows)
            yc = _fourier(f, tab_lat, cs, batch=batch, seq=seq, into=yc)
        h = ffn(h, which=1, mix=(ya, gu, gv, yc, ws, bs, w_out_b),
                final_g=final_norm_g if last else None,
                n_tiles=latent_rows // ROW_TILE if last else None)
    return h.reshape(batch, seq, D_MODEL)
```

```python
import functools
import math

import jax
import jax.numpy as jnp
from jax import lax
from jax.experimental import pallas as pl
from jax.experimental.pallas import tpu as pltpu

D_MODEL = 1024
GRID_W = 64
HEAD_DIM = 64
D_FF = ((8 * D_MODEL // 3 + 127) // 128) * 128
D_LRU = D_MODEL // 2
N_LRU_HEADS = D_LRU // HEAD_DIM
D_GMLP = D_MODEL // 4
N_GMLP_GROUPS = D_GMLP // HEAD_DIM
GMLP_CHUNK = 128
D_FNET = D_MODEL // 4
N_FNET_GROUPS = D_FNET // HEAD_DIM
D_IN = 2 * D_LRU + 2 * D_GMLP + D_FNET
CONV_W = 4
LRU_C = 8.0
N_MOD = 9
EPS = 1e-6

LANES = 128
SUBLANES = 8
VMEM_LIMIT_BYTES = 56 * 1024 * 1024

ROW_TILE = 1024
FFN_PART_ROWS = 512
MIX_IN_TILE = 1024
MIX_IN_PARTS = 4
FF_CHUNK = 256
MOD_ROWS = 16
LRU_GROUP = LANES
N_LRU_GROUPS = D_LRU // LRU_GROUP
N_SEG = SUBLANES
LRU_CHUNK = 64
N_BIAS_ROWS = 2
DFT_Q = 256

BF16 = jnp.bfloat16
F32 = jnp.float32
F32_TINY = float(jnp.finfo(jnp.float32).tiny)
LOG2_E = math.log2(math.e)


def _cparams(sem):
    return pltpu.CompilerParams(dimension_semantics=sem, vmem_limit_bytes=VMEM_LIMIT_BYTES)


def _resident(shape, index_map):
    return pl.BlockSpec(shape, index_map, pipeline_mode=pl.Buffered(1))


def _adaln_kernel(c_ref, w_ref, b_ref, o_ref):
    c = c_ref[...]
    s = (c * jax.nn.sigmoid(c)).astype(BF16)
    o_ref[...] = jnp.dot(s, w_ref[...].astype(BF16), preferred_element_type=F32) + b_ref[...]


def _adaln(cvec, w_mod, b_mod):
    depth = w_mod.shape[0]
    return pl.pallas_call(
        _adaln_kernel,
        grid=(depth, N_MOD),
        in_specs=[
            pl.BlockSpec((MOD_ROWS, D_MODEL), lambda l, j: (0, 0)),
            pl.BlockSpec((None, D_MODEL, D_MODEL), lambda l, j: (l, 0, j)),
            pl.BlockSpec((None, 1, D_MODEL), lambda l, j: (l, 0, j)),
        ],
        out_specs=pl.BlockSpec((None, MOD_ROWS, D_MODEL), lambda l, j: (l, 0, j)),
        out_shape=jax.ShapeDtypeStruct((depth, MOD_ROWS, N_MOD * D_MODEL), F32),
        compiler_params=_cparams(("arbitrary", "arbitrary")),
        name="adaln",
    )(cvec, w_mod, b_mod.reshape(depth, 1, N_MOD * D_MODEL))


def _modulated_norm(h, g, shift, scale):
    y = h * lax.rsqrt(jnp.mean(h * h, axis=-1, keepdims=True) + EPS)
    return ((y * g) * (1.0 + scale) + shift).astype(BF16)


def _mod_spec(layer, tiles_per_seq):
    return pl.BlockSpec((None, None, N_MOD, D_MODEL), lambda i: (layer, i // tiles_per_seq, 0, 0))


def _row_spec(width, tile=ROW_TILE):
    return pl.BlockSpec((tile, width), lambda i: (i, 0))


def _layer_resident(shape, *lead):
    zeros = (0,) * len(shape)
    return _resident((None,) * len(lead) + shape, lambda i: lead + zeros)


def _lru_coefficients(g, xh, c_half):
    t_r = jnp.tanh(g[:, :LRU_GROUP])
    t_i = jnp.tanh(g[:, LRU_GROUP:])
    m = c_half * t_r + c_half
    a = jnp.exp2(m * (-LOG2_E))
    z = jnp.tanh(m) * (a * a + 1.0)
    root = z * lax.rsqrt(jnp.maximum(z, F32_TINY))
    b = (root * xh) * (t_i + 1.0)
    return a, b


def _lru_kernel(xa_ref, ga_ref, cw_ref, cb_ref, wg_ref, lam_ref, h0_ref,
                ya_ref, st_ref, xin_ref, xc_ref, pf_ref, hf_ref, pb_ref, hb_ref, *, seq):
    seg = seq // N_SEG
    chunk = min(LRU_CHUNK, seg)
    blk = chunk * N_SEG
    n_blk = seg // chunk
    v = N_SEG

    for j in range(N_SEG):
        xin_ref[pl.ds(2 * v + j, seg, stride=N_SEG), :] = xa_ref[j * seg:(j + 1) * seg, :]
    zrow = jnp.zeros((1, LRU_GROUP), F32)
    for k in (0, 1):
        tail = xin_ref[(seg + k) * v:(seg + k + 1) * v, :]
        xin_ref[k * v:(k + 1) * v, :] = jnp.concatenate([zrow, tail[:N_SEG - 1]], axis=0)
    head = xin_ref[2 * v:3 * v, :]
    xin_ref[(seg + 2) * v:(seg + 3) * v, :] = jnp.concatenate([head[1:], zrow], axis=0)

    cw = 0.5 * cw_ref[...]
    cb = 0.5 * cb_ref[...]
    for c in range(n_blk):
        acc = cb
        for k in range(CONV_W):
            acc = acc + xin_ref[c * blk + k * v:c * blk + k * v + blk, :] * cw[k:k + 1, :]
        xc_ref[c * blk:(c + 1) * blk, :] = acc

    c_half = (0.5 * LRU_C) * jax.nn.softplus(-lam_ref[...])
    lane = lax.broadcasted_iota(jnp.int32, (blk, LRU_GROUP), 1)
    ones = jnp.where(lane < N_BIAS_ROWS, 1.0, 0.0).astype(BF16)

    def block(c, d, state, p_ref, h_ref):
        rows = slice(c * blk, (c + 1) * blk)
        xh = xc_ref[rows, :]
        cols = slice(2 * d * LRU_GROUP, (2 * d + 2) * LRU_GROUP)
        g = jnp.dot(jnp.concatenate([xh.astype(BF16), ones], axis=1), wg_ref[:, cols],
                    preferred_element_type=F32)
        a, b = _lru_coefficients(g, xh, c_half[d:d + 1, :])
        p, h = state
        ps, hs = [], []
        steps = range(chunk) if d == 0 else reversed(range(chunk))
        for s in steps:
            a_s = a[s * v:(s + 1) * v, :]
            p = a_s * p
            h = a_s * h + b[s * v:(s + 1) * v, :]
            ps.append(p)
            hs.append(h)
        if d == 1:
            ps, hs = ps[::-1], hs[::-1]
        p_ref[rows, :] = jnp.concatenate(ps, axis=0)
        h_ref[rows, :] = jnp.concatenate(hs, axis=0)
        return p, h

    one = jnp.ones((N_SEG, LRU_GROUP), F32)
    zero = jnp.zeros((N_SEG, LRU_GROUP), F32)
    fwd = bwd = (one, zero)
    for c in range(n_blk):
        fwd = block(c, 0, fwd, pf_ref, hf_ref)
        bwd = block(n_blk - 1 - c, 1, bwd, pb_ref, hb_ref)
    (pf, hf), (pb, hb) = fwd, bwd

    h0 = h0_ref[...]
    c = h0[0:1, :]
    rows = []
    for j in range(N_SEG):
        rows.append(c)
        c = hf[j:j + 1, :] + pf[j:j + 1, :] * c
    carry_f = jnp.concatenate(rows * chunk, axis=0)
    final_f = c
    c = h0[1:2, :]
    rows = []
    for j in reversed(range(N_SEG)):
        rows.append(c)
        c = hb[j:j + 1, :] + pb[j:j + 1, :] * c
    carry_b = jnp.concatenate(rows[::-1] * chunk, axis=0)
    final_b = c
    st_ref[...] = jnp.concatenate([final_f, final_b], axis=0)

    for c in range(n_blk):
        rows = slice(c * blk, (c + 1) * blk)
        xin_ref[rows, :] = ((hf_ref[rows, :] + pf_ref[rows, :] * carry_f)
                            + (hb_ref[rows, :] + pb_ref[rows, :] * carry_b))
    for j in range(N_SEG):
        y = xin_ref[pl.ds(j, seg, stride=N_SEG), :]
        ga = ga_ref[j * seg:(j + 1) * seg, :].astype(F32)
        ya_ref[j * seg:(j + 1) * seg, :] = (y * ga).astype(BF16)


def _lru(xa, ga, conv_w, conv_b, wg, lam, h0, *, layer, batch, seq, first_row=0):
    seg = seq // N_SEG
    first_block = first_row // seq
    assert first_row == first_block * seq
    src = lambda: pl.BlockSpec((seq, LRU_GROUP), lambda b, g: (first_block + b, g))
    grp = lambda r: pl.BlockSpec((None, r, LRU_GROUP), lambda b, g: (layer, 0, g))
    st = lambda: pl.BlockSpec((None, 2, LRU_GROUP), lambda b, g: (b, 0, g))
    return pl.pallas_call(
        functools.partial(_lru_kernel, seq=seq),
        grid=(batch, N_LRU_GROUPS),
        in_specs=[
            src(), src(), grp(CONV_W), grp(1),
            pl.BlockSpec((None, None, 2 * LRU_GROUP, 4 * LRU_GROUP), lambda b, g: (layer, g, 0, 0)),
            grp(2), st(),
        ],
        out_specs=[pl.BlockSpec((seq, LRU_GROUP), lambda b, g: (b, g)), st()],
        out_shape=[
            jax.ShapeDtypeStruct((batch * seq, D_LRU), BF16),
            jax.ShapeDtypeStruct((batch, 2, D_LRU), F32),
        ],
        scratch_shapes=[pltpu.VMEM(((seg + CONV_W - 1) * N_SEG, LRU_GROUP), F32)]
        + [pltpu.VMEM((seq, LRU_GROUP), F32) for _ in range(5)],
        compiler_params=_cparams(("arbitrary", "arbitrary")),
        name="lru",
    )(xa, ga, conv_w, conv_b, wg, lam, h0)


def _lru_gate_weights(w_gates, b_gates):
    depth = w_gates.shape[0]
    heads_per_group = LRU_GROUP // HEAD_DIM
    w = w_gates.astype(BF16).reshape(depth, 2, 2, N_LRU_GROUPS, heads_per_group, HEAD_DIM, HEAD_DIM)
    rows = [jnp.pad(w[:, :, :, :, k], [(0, 0)] * 5 + [(k * HEAD_DIM, LRU_GROUP - (k + 1) * HEAD_DIM)])
            for k in range(heads_per_group)]
    bd = jnp.concatenate(rows, axis=-2)
    wg = jnp.transpose(bd, (0, 3, 4, 1, 2, 5)).reshape(depth, N_LRU_GROUPS, LRU_GROUP, 4 * LRU_GROUP)
    b = 0.5 * b_gates.reshape(depth, 2, 2, N_LRU_GROUPS, LRU_GROUP)
    b = jnp.transpose(b, (0, 3, 1, 2, 4)).reshape(depth, N_LRU_GROUPS, 1, 4 * LRU_GROUP)
    b_hi = b.astype(BF16)
    b_lo = (b - b_hi.astype(F32)).astype(BF16)
    assert N_BIAS_ROWS == 2
    pad = jnp.zeros((depth, N_LRU_GROUPS, LRU_GROUP - N_BIAS_ROWS, 4 * LRU_GROUP), BF16)
    return jnp.concatenate([wg, b_hi, b_lo, pad], axis=2)


def _mixer_output(h, gate, row0, ya, gu_ref, gv_ref, yc, ws_ref, bs_ref, wo_ref):
    tm = h.shape[0]
    half = D_GMLP // 2
    lane = lax.broadcasted_iota(jnp.int32, (GMLP_CHUNK, half), 1)
    first_group = lane < HEAD_DIM
    yb = []
    for c in range(tm // GMLP_CHUNK):
        rows = slice(row0 + c * GMLP_CHUNK, row0 + (c + 1) * GMLP_CHUNK)
        cols = []
        for p in range(2):
            m = jnp.dot(ws_ref[p * 2 * GMLP_CHUNK:(p + 1) * 2 * GMLP_CHUNK, :],
                        gv_ref[rows, p * half:(p + 1) * half], preferred_element_type=F32)
            cols.append(jnp.where(first_group, m[:GMLP_CHUNK], m[GMLP_CHUNK:]))
        mixed = jnp.concatenate(cols, axis=1) + bs_ref[...]
        yb.append((gu_ref[rows, :].astype(F32) * mixed).astype(BF16))
    yb = jnp.concatenate(yb, axis=0)
    y = jnp.dot(ya, wo_ref[0:D_LRU, :], preferred_element_type=F32)
    y = y + jnp.dot(yb, wo_ref[D_LRU:D_LRU + D_GMLP, :], preferred_element_type=F32)
    y = y + jnp.dot(yc, wo_ref[D_LRU + D_GMLP:, :], preferred_element_type=F32)
    return h + gate * y


def _ffn_kernel(*refs, k0, embed, mix, final_norm, parts):
    refs = list(refs)
    take = lambda n: [refs.pop(0) for _ in range(n)]
    (h_ref,) = take(1)
    ctx_ref, tab_ref = take(2) if embed else (None, None)
    (mod_ref,) = take(1)
    ya_ref, yc_ref = take(2) if mix else (None, None)
    yac_ref, ycc_ref = take(2) if mix > 1 else (None, None)
    mix_refs = take(5) if mix else None
    g_ref, wgu_ref, wd_ref = take(3)
    (gf_ref,) = take(1) if final_norm else (None,)
    (o_ref,) = take(1)
    assert not refs

    tile = h_ref.shape[0]
    tm = tile // parts
    shift = mod_ref[k0:k0 + 1, :]
    scale = mod_ref[k0 + 1:k0 + 2, :]
    gate = mod_ref[k0 + 2:k0 + 3, :]
    for part in range(parts):
        row0 = part * tm
        h = h_ref[row0:row0 + tm, :]
        if embed:
            tiles_per_seq, n_latent_tiles = embed
            n_grid_rows = tm // GRID_W
            r0 = (pl.program_id(0) % tiles_per_seq) * (tile // GRID_W) + part * n_grid_rows
            tab = tab_ref[...]
            row_part = jnp.concatenate(
                [jnp.broadcast_to(tab_ref[pl.ds(r0 + a, 1), :], (GRID_W, D_MODEL // 2))
                 for a in range(n_grid_rows)], axis=0)
            col_part = jnp.concatenate([tab] * n_grid_rows, axis=0)
            h = jnp.where(pl.program_id(0) < n_latent_tiles,
                          h + jnp.concatenate([row_part, col_part], axis=1),
                          ctx_ref[row0:row0 + tm, :])
        if mix:
            ya, yc = ya_ref[row0:row0 + tm, :], yc_ref[row0:row0 + tm, :]
            if mix > 1:
                is_latent = pl.program_id(0) < mix - 1
                ya = jnp.where(is_latent, ya, yac_ref[row0:row0 + tm, :])
                yc = jnp.where(is_latent, yc, ycc_ref[row0:row0 + tm, :])
            gu_ref, gv_ref, ws_ref, bs_ref, wo_ref = mix_refs
            h = _mixer_output(h, mod_ref[5:6, :], row0, ya, gu_ref, gv_ref, yc, ws_ref, bs_ref, wo_ref)

        n = _modulated_norm(h, g_ref[...], shift, scale)
        act = []
        for j in range(D_FF // FF_CHUNK):
            lo = j * FF_CHUNK
            gt = jnp.dot(n, wgu_ref[:, lo:lo + FF_CHUNK], preferred_element_type=F32)
            up = jnp.dot(n, wgu_ref[:, D_FF + lo:D_FF + lo + FF_CHUNK], preferred_element_type=F32)
            act.append((gt * jax.nn.sigmoid(gt) * up).astype(BF16))
        acc = jnp.dot(jnp.concatenate(act, axis=1), wd_ref[...], preferred_element_type=F32)
        out = h + 0.5 * gate * acc
        if final_norm:
            out = out * lax.rsqrt(jnp.mean(out * out, axis=-1, keepdims=True) + EPS) * gf_ref[...]
        o_ref[row0:row0 + tm, :] = out


def _ffn(h, mod, norm_g, wgu, wd, *, layer, which, tiles_per_seq, embed=None, mix=None,
         final_g=None, n_tiles=None, tile=ROW_TILE):
    rows = h.shape[0]
    rspec = functools.partial(_row_spec, tile=tile)
    in_specs = [rspec(D_MODEL)]
    args = [h]
    embed_static = 0
    if embed is not None:
        ctx, pos_tab = embed
        n_latent_tiles = rows // tile
        embed_static = (tiles_per_seq, n_latent_tiles)
        rows += ctx.shape[0]
        in_specs = [
            pl.BlockSpec((tile, D_MODEL), lambda i: (jnp.minimum(i, n_latent_tiles - 1), 0)),
            pl.BlockSpec((tile, D_MODEL), lambda i: (jnp.maximum(i - n_latent_tiles, 0), 0)),
            _resident((GRID_W, D_MODEL // 2), lambda i: (0, 0)),
        ]
        args += [ctx, pos_tab]
    n_tiles = rows // tile if n_tiles is None else n_tiles
    in_specs.append(_mod_spec(layer, tiles_per_seq))
    args.append(mod)
    mix_static = 0
    if mix is not None:
        ya, yc, ya_ctx, yc_ctx, gu, gv, ws, bs, w_out = mix
        n_latent = ya.shape[0] // tile
        latent = lambda w: pl.BlockSpec((tile, w), lambda i: (jnp.minimum(i, n_latent - 1), 0))
        context = lambda w: pl.BlockSpec((tile, w), lambda i: (jnp.maximum(i - n_latent, 0), 0))
        in_specs += [latent(D_LRU), latent(D_FNET)]
        args += [ya, yc]
        mix_static = 1
        if ya_ctx is not None:
            in_specs += [context(D_LRU), context(D_FNET)]
            args += [ya_ctx, yc_ctx]
            mix_static = 1 + n_latent
        in_specs += [
            rspec(D_GMLP), rspec(D_GMLP),
            _layer_resident((N_GMLP_GROUPS * GMLP_CHUNK, GMLP_CHUNK), layer),
            _layer_resident((GMLP_CHUNK, D_GMLP), layer),
            _layer_resident((D_MODEL, D_MODEL), layer),
        ]
        args += [gu, gv, ws, bs, w_out]
    in_specs += [
        _layer_resident((1, D_MODEL), layer, 2 * which),
        _layer_resident((D_MODEL, 2 * D_FF), layer, which),
        _layer_resident((D_FF, D_MODEL), layer, which),
    ]
    args += [norm_g, wgu, wd]
    if final_g is not None:
        in_specs.append(_resident((1, D_MODEL), lambda i: (0, 0)))
        args.append(final_g.reshape(1, D_MODEL))
    kern = functools.partial(_ffn_kernel, k0=6 * which, embed=embed_static,
                             mix=mix_static, final_norm=final_g is not None,
                             parts=tile // FFN_PART_ROWS)
    return pl.pallas_call(
        kern,
        grid=(n_tiles,),
        in_specs=in_specs,
        out_specs=rspec(D_MODEL),
        out_shape=jax.ShapeDtypeStruct((n_tiles * tile, D_MODEL), F32),
        compiler_params=_cparams(("arbitrary",)),
        name="ffn",
    )(*args)


def _mix_in_kernel(h_ref, mod_ref, g_ref, w_ref, xa_ref, ga_ref, gu_ref, gv_ref, f_ref, fc_ref,
                   fs_ref, *, radix, n_latent_tiles):
    tile = h_ref.shape[0]
    n_slabs = D_FNET // LANES
    for part in range(MIX_IN_PARTS):
        rows = slice(part * tile // MIX_IN_PARTS, (part + 1) * tile // MIX_IN_PARTS)
        n = _modulated_norm(h_ref[rows, :], g_ref[...], mod_ref[3:4, :], mod_ref[4:5, :])

        def proj(lo, width):
            return jnp.dot(n, w_ref[:, lo:lo + width], preferred_element_type=F32)

        xa_ref[rows, :] = proj(0, D_LRU)
        ga_ref[rows, :] = jax.nn.gelu(proj(D_LRU, D_LRU)).astype(BF16)
        gu_ref[rows, :] = jax.nn.gelu(proj(2 * D_LRU, D_GMLP)).astype(BF16)
        gv_ref[rows, :] = jax.nn.gelu(proj(2 * D_LRU + D_GMLP, D_GMLP)).astype(BF16)
        f = proj(2 * D_LRU + 2 * D_GMLP, D_FNET)
        for s in range(n_slabs):
            fs_ref[s, rows, :] = f[:, s * LANES:(s + 1) * LANES]

    is_latent = pl.program_id(0) < n_latent_tiles

    @pl.when(is_latent)
    def _():
        per_class = tile // radix
        for a in range(radix):
            piece = [fs_ref[s, pl.ds(a, per_class, stride=radix), :] for s in range(n_slabs)]
            f_ref[a] = jnp.concatenate(piece, axis=1).astype(BF16)

    @pl.when(jnp.logical_not(is_latent))
    def _():
        fc_ref[...] = jnp.concatenate([fs_ref[s] for s in range(n_slabs)], axis=1).astype(BF16)


def _mix_in(h, mod, norm_g, w_in, *, layer, radix, tiles_per_seq, n_latent_tiles):
    rows = h.shape[0]
    tile = MIX_IN_TILE
    n_tiles = rows // tile
    rspec = functools.partial(_row_spec, tile=tile)
    per_class = tile // radix
    latent = lambda i: jnp.minimum(i, n_latent_tiles - 1)
    return pl.pallas_call(
        functools.partial(_mix_in_kernel, radix=radix, n_latent_tiles=n_latent_tiles),
        grid=(n_tiles,),
        in_specs=[
            rspec(D_MODEL),
            _mod_spec(layer, tiles_per_seq),
            _layer_resident((1, D_MODEL), layer, 1),
            _layer_resident((D_MODEL, D_IN), layer),
        ],
        out_specs=[
            rspec(D_LRU), rspec(D_LRU), rspec(D_GMLP), rspec(D_GMLP),
            pl.BlockSpec((None, radix, per_class, D_FNET),
                         lambda i: (latent(i) // tiles_per_seq, 0, latent(i) % tiles_per_seq, 0)),
            pl.BlockSpec((tile, D_FNET), lambda i: (jnp.maximum(i - n_latent_tiles, 0), 0)),
        ],
        out_shape=[
            jax.ShapeDtypeStruct((rows, D_LRU), F32),
            jax.ShapeDtypeStruct((rows, D_LRU), BF16),
            jax.ShapeDtypeStruct((rows, D_GMLP), BF16),
            jax.ShapeDtypeStruct((rows, D_GMLP), BF16),
            jax.ShapeDtypeStruct((n_latent_tiles // tiles_per_seq, radix, tiles_per_seq * per_class,
                                  D_FNET), BF16),
            jax.ShapeDtypeStruct(((n_tiles - n_latent_tiles) * tile, D_FNET), BF16),
        ],
        scratch_shapes=[pltpu.VMEM((D_FNET // LANES, tile, LANES), F32)],
        compiler_params=_cparams(("arbitrary",)),
        name="mix_in",
    )(h, mod, norm_g, w_in)


def _fft(xs):
    n = len(xs)
    if n == 1:
        return xs
    even, odd = _fft(xs[0::2]), _fft(xs[1::2])
    out = [None] * n
    for k in range(n // 2):
        (er, ei), (qr, qi) = even[k], odd[k]
        if k == 0:
            tr, ti = qr, qi
        elif 4 * k == n:
            tr, ti = qi, -qr
        else:
            wr, wi = math.cos(2 * math.pi * k / n), -math.sin(2 * math.pi * k / n)
            tr, ti = wr * qr - wi * qi, wr * qi + wi * qr
        out[k] = (er + tr, ei + ti)
        out[k + n // 2] = (er - tr, ei - ti)
    return out


def _fourier_kernel(x_ref, t_ref, cs_ref, o_ref, u_ref, p_ref, *, radix):
    q = DFT_Q
    n_half = D_FNET // LANES
    for a in range(radix):
        u = jnp.dot(t_ref[a], x_ref[a], preferred_element_type=F32)
        for part in range(2):
            for s in range(n_half):
                dst = p_ref if radix == 1 else u_ref.at[a]
                dst[part * n_half + s] = u[part * q:(part + 1) * q, s * LANES:(s + 1) * LANES]

    if radix > 1:
        def butterflies(i, _):
            s = i // (q // SUBLANES)
            r = pl.multiple_of((i % (q // SUBLANES)) * SUBLANES, SUBLANES)
            xs = [(u_ref[a, s, pl.ds(r, SUBLANES), :], u_ref[a, n_half + s, pl.ds(r, SUBLANES), :])
                  for a in range(radix)]
            for c, (pr, pi) in enumerate(_fft(xs)):
                p_ref[s, pl.ds(c * q + r, SUBLANES), :] = pr
                p_ref[n_half + s, pl.ds(c * q + r, SUBLANES), :] = pi
            return 0

        lax.fori_loop(0, n_half * (q // SUBLANES), butterflies, 0, unroll=2)

    for c in range(radix):
        rows = slice(c * q, (c + 1) * q)
        p = jnp.concatenate([p_ref[s, rows, :] for s in range(2 * n_half)], axis=1).astype(BF16)
        o_ref[rows, :] = jnp.dot(p, cs_ref[...], preferred_element_type=F32).astype(BF16)


def _fourier(f, tables, cs, *, batch, seq):
    radix = seq // DFT_Q
    n_slab = 2 * (D_FNET // LANES)
    return pl.pallas_call(
        functools.partial(_fourier_kernel, radix=radix),
        grid=(batch,),
        in_specs=[
            pl.BlockSpec((None, radix, DFT_Q, D_FNET), lambda b: (b, 0, 0, 0)),
            _resident((radix, 2 * DFT_Q, DFT_Q), lambda b: (0, 0, 0)),
            _resident((2 * D_FNET, D_FNET), lambda b: (0, 0)),
        ],
        out_specs=pl.BlockSpec((seq, D_FNET), lambda b: (b, 0)),
        out_shape=jax.ShapeDtypeStruct((batch * seq, D_FNET), BF16),
        scratch_shapes=[pltpu.VMEM((radix, n_slab, DFT_Q, LANES), F32),
                        pltpu.VMEM((n_slab, seq, LANES), F32)],
        compiler_params=_cparams(("arbitrary",)),
        name="fourier",
    )(f.reshape(batch, radix, DFT_Q, D_FNET), tables, cs)


def _position_dft_tables(seq):
    radix = seq // DFT_Q
    scale = 1.0 / math.sqrt(seq * HEAD_DIM)
    a = jnp.arange(radix, dtype=jnp.int32)[:, None]
    d = jnp.arange(DFT_Q, dtype=jnp.int32)
    ang_a = ((d[None, :] * a) % seq).astype(F32) * (2.0 * math.pi / seq)
    ang_b = ((d[:, None] * d[None, :]) % DFT_Q).astype(F32) * (2.0 * math.pi / DFT_Q)
    ca, sa = (jnp.cos(ang_a) * scale)[:, :, None], (jnp.sin(ang_a) * scale)[:, :, None]
    cb, sb = jnp.cos(ang_b)[None], jnp.sin(ang_b)[None]
    cos = ca * cb - sa * sb
    msin = -(sa * cb + ca * sb)
    return jnp.concatenate([cos, msin], axis=1).astype(BF16)


def _channel_dft_matrix():
    c = jnp.arange(D_FNET, dtype=jnp.int32)
    same_group = (c[:, None] // HEAD_DIM) == (c[None, :] // HEAD_DIM)
    ang = (((c[:, None] % HEAD_DIM) * (c[None, :] % HEAD_DIM)) % HEAD_DIM).astype(F32) * (
        2.0 * math.pi / HEAD_DIM)
    cs = jnp.concatenate([jnp.where(same_group, jnp.cos(ang), 0.0),
                          jnp.where(same_group, jnp.sin(ang), 0.0)], axis=0)
    return cs.astype(BF16)


def _pos_table_kernel(f_ref, o_ref):
    j = lax.broadcasted_iota(jnp.int32, (GRID_W, D_MODEL // 4), 0).astype(F32)
    e = j * f_ref[...]
    o_ref[...] = jnp.concatenate([jnp.sin(e), jnp.cos(e)], axis=1)


def _pos_table():
    q = D_MODEL // 4
    freqs = 1.0 / (10000.0 ** (jnp.arange(q, dtype=F32) / q))
    return pl.pallas_call(
        _pos_table_kernel,
        out_shape=jax.ShapeDtypeStruct((GRID_W, D_MODEL // 2), F32),
        name="pos_table",
    )(freqs.reshape(1, q))


def kernel(x, c, ctx, c_ctx, w_mod, b_mod, norm_g, ffn_w_gu, ffn_w_down, w_in, w_out, conv_w, conv_b,
           lru_w_gates, lru_b_gates, lru_lambda, gmlp_ws, gmlp_bs, final_norm_g):
    batch, seq, _ = x.shape
    ctx_len = ctx.shape[1]
    depth = w_mod.shape[0]
    assert batch < MOD_ROWS and seq % ROW_TILE == 0 and (batch * ctx_len) % ROW_TILE == 0
    assert seq % (N_SEG * LRU_CHUNK) == 0 and ctx_len % (N_SEG * SUBLANES) == 0
    assert LRU_CHUNK % (ctx_len // N_SEG) == 0 or (ctx_len // N_SEG) % LRU_CHUNK == 0
    radix = seq // DFT_Q
    assert seq == radix * DFT_Q and radix & (radix - 1) == 0 and ROW_TILE % radix == 0
    assert ctx_len == DFT_Q
    assert seq % MIX_IN_TILE == 0 and (batch * ctx_len) % MIX_IN_TILE == 0 and MIX_IN_TILE % radix == 0
    ctx_row = batch
    latent_rows = batch * seq
    tiles = dict(tiles_per_seq=seq // ROW_TILE)

    cvec = jnp.zeros((MOD_ROWS, D_MODEL), F32).at[:batch].set(c).at[ctx_row].set(c_ctx)
    mod = _adaln(cvec, w_mod, b_mod).reshape(depth, MOD_ROWS, N_MOD, D_MODEL)

    wgu = ffn_w_gu.astype(BF16)
    wd = ffn_w_down.astype(BF16)
    w_in_b = w_in.astype(BF16)
    w_out_b = w_out.astype(BF16)
    norm_g4 = norm_g.reshape(depth, 3, 1, D_MODEL)
    ws = gmlp_ws.astype(BF16).reshape(depth, N_GMLP_GROUPS * GMLP_CHUNK, GMLP_CHUNK)
    bs = jnp.repeat(jnp.swapaxes(gmlp_bs, 1, 2), HEAD_DIM, axis=2)
    wg = _lru_gate_weights(lru_w_gates, lru_b_gates)
    conv_b3 = conv_b.reshape(depth, 1, D_LRU)
    cs = _channel_dft_matrix()
    tab_lat = _position_dft_tables(seq)
    tab_ctx = _position_dft_tables(ctx_len)
    pos_tab = _pos_table()

    h = x.reshape(latent_rows, D_MODEL)
    zero_state = jnp.zeros((batch, 2, D_LRU), F32)

    for l in range(depth):
        last = l == depth - 1
        ffn = functools.partial(_ffn, mod=mod, norm_g=norm_g4, wgu=wgu, wd=wd, layer=l, **tiles)
        lru = functools.partial(_lru, conv_w=conv_w, conv_b=conv_b3, wg=wg, lam=lru_lambda,
                                layer=l, batch=batch)

        h = ffn(h, which=0,
                embed=(ctx.reshape(batch * ctx_len, D_MODEL), pos_tab) if l == 0 else None)
        xa, ga, gu, gv, f, f_c = _mix_in(h, mod, norm_g4, w_in_b, layer=l, radix=radix,
                                         tiles_per_seq=seq // MIX_IN_TILE,
                                         n_latent_tiles=latent_rows // MIX_IN_TILE)
        ya_c, state_c = lru(xa, ga, h0=zero_state, seq=ctx_len, first_row=latent_rows)
        ya, _ = lru(xa, ga, h0=state_c, seq=seq)
        yc = _fourier(f, tab_lat, cs, batch=batch, seq=seq)
        if last:
            ya_c = yc_c = None
        else:
            yc_c = _fourier(f_c, tab_ctx, cs, batch=batch, seq=ctx_len)
        h = ffn(h, which=1, mix=(ya, yc, ya_c, yc_c, gu, gv, ws, bs, w_out_b),
                final_g=final_norm_g if last else None,
                n_tiles=latent_rows // ROW_TILE if last else None)
    return h.reshape(batch, seq, D_MODEL)
```

```python
import functools
import math

import jax
import jax.numpy as jnp
from jax import lax
from jax.experimental import pallas as pl
from jax.experimental.pallas import tpu as pltpu

D_MODEL = 1024
GRID_W = 64
HEAD_DIM = 64
D_FF = ((8 * D_MODEL // 3 + 127) // 128) * 128
D_LRU = D_MODEL // 2
N_LRU_HEADS = D_LRU // HEAD_DIM
D_GMLP = D_MODEL // 4
N_GMLP_GROUPS = D_GMLP // HEAD_DIM
GMLP_CHUNK = 128
D_FNET = D_MODEL // 4
N_FNET_GROUPS = D_FNET // HEAD_DIM
D_IN = 2 * D_LRU + 2 * D_GMLP + D_FNET
CONV_W = 4
LRU_C = 8.0
N_MOD = 9
EPS = 1e-6

LANES = 128
SUBLANES = 8
VMEM_LIMIT_BYTES = 56 * 1024 * 1024

ROW_TILE = 1024
FFN_PART_ROWS = 512
MIX_IN_TILE = 1024
MIX_IN_PARTS = 4
FF_CHUNK = 256
MOD_ROWS = 16
LRU_GROUP = LANES
N_LRU_GROUPS = D_LRU // LRU_GROUP
N_SEG = SUBLANES
LRU_CHUNK = 64
LRU_STEP_ROWS = 1024
N_BIAS_ROWS = 2
DFT_Q = 256

BF16 = jnp.bfloat16
F32 = jnp.float32
F32_TINY = float(jnp.finfo(jnp.float32).tiny)
LOG2_E = math.log2(math.e)


def _cparams(sem):
    return pltpu.CompilerParams(dimension_semantics=sem, vmem_limit_bytes=VMEM_LIMIT_BYTES)


def _resident(shape, index_map):
    return pl.BlockSpec(shape, index_map, pipeline_mode=pl.Buffered(1))


def _adaln_kernel(c_ref, w_ref, b_ref, o_ref):
    c = c_ref[...]
    s = (c * jax.nn.sigmoid(c)).astype(BF16)
    o_ref[...] = jnp.dot(s, w_ref[...].astype(BF16), preferred_element_type=F32) + b_ref[...]


def _adaln(cvec, w_mod, b_mod):
    depth = w_mod.shape[0]
    return pl.pallas_call(
        _adaln_kernel,
        grid=(depth, N_MOD),
        in_specs=[
            pl.BlockSpec((MOD_ROWS, D_MODEL), lambda l, j: (0, 0)),
            pl.BlockSpec((None, D_MODEL, D_MODEL), lambda l, j: (l, 0, j)),
            pl.BlockSpec((None, 1, D_MODEL), lambda l, j: (l, 0, j)),
        ],
        out_specs=pl.BlockSpec((None, MOD_ROWS, D_MODEL), lambda l, j: (l, 0, j)),
        out_shape=jax.ShapeDtypeStruct((depth, MOD_ROWS, N_MOD * D_MODEL), F32),
        compiler_params=_cparams(("arbitrary", "arbitrary")),
        name="adaln",
    )(cvec, w_mod, b_mod.reshape(depth, 1, N_MOD * D_MODEL))


def _modulated_norm(h, g, shift, scale):
    y = h * lax.rsqrt(jnp.mean(h * h, axis=-1, keepdims=True) + EPS)
    return ((y * g) * (1.0 + scale) + shift).astype(BF16)


def _mod_spec(layer, tiles_per_seq):
    return pl.BlockSpec((None, None, N_MOD, D_MODEL), lambda i: (layer, i // tiles_per_seq, 0, 0))


def _row_spec(width, tile=ROW_TILE):
    return pl.BlockSpec((tile, width), lambda i: (i, 0))


def _layer_resident(shape, *lead):
    zeros = (0,) * len(shape)
    return _resident((None,) * len(lead) + shape, lambda i: lead + zeros)


def _lru_coefficients(g, xh, c_half):
    t_r = jnp.tanh(g[:, :LRU_GROUP])
    t_i = jnp.tanh(g[:, LRU_GROUP:])
    m = c_half * t_r + c_half
    a = jnp.exp2(m * (-LOG2_E))
    z = jnp.tanh(m) * (a * a + 1.0)
    root = z * lax.rsqrt(jnp.maximum(z, F32_TINY))
    b = (root * xh) * (t_i + 1.0)
    return a, b


def _lru_kernel(xa_ref, ga_ref, cw_ref, cb_ref, wg_ref, lam_ref, h0_ref,
                ya_ref, st_ref, xin_ref, xc_ref, pf_ref, hf_ref, pb_ref, hb_ref, *, seq):
    seg = seq // N_SEG
    chunk = min(LRU_CHUNK, seg)
    blk = chunk * N_SEG
    n_blk = seg // chunk
    v = N_SEG

    for j in range(N_SEG):
        xin_ref[pl.ds(2 * v + j, seg, stride=N_SEG), :] = xa_ref[j * seg:(j + 1) * seg, :]
    zrow = jnp.zeros((1, LRU_GROUP), F32)
    for k in (0, 1):
        tail = xin_ref[(seg + k) * v:(seg + k + 1) * v, :]
        xin_ref[k * v:(k + 1) * v, :] = jnp.concatenate([zrow, tail[:N_SEG - 1]], axis=0)
    head = xin_ref[2 * v:3 * v, :]
    xin_ref[(seg + 2) * v:(seg + 3) * v, :] = jnp.concatenate([head[1:], zrow], axis=0)

    cw = 0.5 * cw_ref[...]
    cb = 0.5 * cb_ref[...]
    for c in range(n_blk):
        acc = cb
        for k in range(CONV_W):
            acc = acc + xin_ref[c * blk + k * v:c * blk + k * v + blk, :] * cw[k:k + 1, :]
        xc_ref[c * blk:(c + 1) * blk, :] = acc

    c_half = (0.5 * LRU_C) * jax.nn.softplus(-lam_ref[...])
    lane = lax.broadcasted_iota(jnp.int32, (blk, LRU_GROUP), 1)
    ones = jnp.where(lane < N_BIAS_ROWS, 1.0, 0.0).astype(BF16)

    def block(c, d, state, p_ref, h_ref):
        rows = slice(c * blk, (c + 1) * blk)
        xh = xc_ref[rows, :]
        cols = slice(2 * d * LRU_GROUP, (2 * d + 2) * LRU_GROUP)
        g = jnp.dot(jnp.concatenate([xh.astype(BF16), ones], axis=1), wg_ref[:, cols],
                    preferred_element_type=F32)
        a, b = _lru_coefficients(g, xh, c_half[d:d + 1, :])
        p, h = state
        ps, hs = [], []
        steps = range(chunk) if d == 0 else reversed(range(chunk))
        for s in steps:
            a_s = a[s * v:(s + 1) * v, :]
            p = a_s * p
            h = a_s * h + b[s * v:(s + 1) * v, :]
            ps.append(p)
            hs.append(h)
        if d == 1:
            ps, hs = ps[::-1], hs[::-1]
        p_ref[rows, :] = jnp.concatenate(ps, axis=0)
        h_ref[rows, :] = jnp.concatenate(hs, axis=0)
        return p, h

    one = jnp.ones((N_SEG, LRU_GROUP), F32)
    zero = jnp.zeros((N_SEG, LRU_GROUP), F32)
    fwd = bwd = (one, zero)
    for c in range(n_blk):
        fwd = block(c, 0, fwd, pf_ref, hf_ref)
        bwd = block(n_blk - 1 - c, 1, bwd, pb_ref, hb_ref)
    (pf, hf), (pb, hb) = fwd, bwd

    h0 = h0_ref[...]
    c = h0[0:1, :]
    rows = []
    for j in range(N_SEG):
        rows.append(c)
        c = hf[j:j + 1, :] + pf[j:j + 1, :] * c
    carry_f = jnp.concatenate(rows * chunk, axis=0)
    final_f = c
    c = h0[1:2, :]
    rows = []
    for j in reversed(range(N_SEG)):
        rows.append(c)
        c = hb[j:j + 1, :] + pb[j:j + 1, :] * c
    carry_b = jnp.concatenate(rows[::-1] * chunk, axis=0)
    final_b = c
    st_ref[...] = jnp.concatenate([final_f, final_b], axis=0)

    for c in range(n_blk):
        rows = slice(c * blk, (c + 1) * blk)
        xin_ref[rows, :] = ((hf_ref[rows, :] + pf_ref[rows, :] * carry_f)
                            + (hb_ref[rows, :] + pb_ref[rows, :] * carry_b))
    for j in range(N_SEG):
        y = xin_ref[pl.ds(j, seg, stride=N_SEG), :]
        ga = ga_ref[j * seg:(j + 1) * seg, :].astype(F32)
        ya_ref[j * seg:(j + 1) * seg, :] = (y * ga).astype(BF16)


def _lru_groups_kernel(xa_ref, ga_ref, cw_ref, cb_ref, wg_ref, lam_ref, h0_ref,
                       ya_ref, st_ref, *scratch, seq, groups):
    for k in range(groups):
        cols = slice(k * LRU_GROUP, (k + 1) * LRU_GROUP)
        _lru_kernel(xa_ref.at[:, cols], ga_ref.at[:, cols], cw_ref.at[:, cols], cb_ref.at[:, cols],
                    wg_ref.at[k], lam_ref.at[:, cols], h0_ref.at[:, cols],
                    ya_ref.at[:, cols], st_ref.at[:, cols], *scratch, seq=seq)


def _lru(xa, ga, conv_w, conv_b, wg, lam, h0, *, layer, batch, seq, first_row=0):
    seg = seq // N_SEG
    first_block = first_row // seq
    assert first_row == first_block * seq
    groups = min(N_LRU_GROUPS, max(1, LRU_STEP_ROWS // seq))
    assert N_LRU_GROUPS % groups == 0
    width = groups * LRU_GROUP
    src = lambda: pl.BlockSpec((seq, width), lambda b, g: (first_block + b, g))
    grp = lambda r: pl.BlockSpec((None, r, width), lambda b, g: (layer, 0, g))
    st = lambda: pl.BlockSpec((None, 2, width), lambda b, g: (b, 0, g))
    return pl.pallas_call(
        functools.partial(_lru_groups_kernel, seq=seq, groups=groups),
        grid=(batch, N_LRU_GROUPS // groups),
        in_specs=[
            src(), src(), grp(CONV_W), grp(1),
            pl.BlockSpec((None, groups, 2 * LRU_GROUP, 4 * LRU_GROUP),
                         lambda b, g: (layer, g, 0, 0)),
            grp(2), st(),
        ],
        out_specs=[pl.BlockSpec((seq, width), lambda b, g: (b, g)), st()],
        out_shape=[
            jax.ShapeDtypeStruct((batch * seq, D_LRU), BF16),
            jax.ShapeDtypeStruct((batch, 2, D_LRU), F32),
        ],
        scratch_shapes=[pltpu.VMEM(((seg + CONV_W - 1) * N_SEG, LRU_GROUP), F32)]
        + [pltpu.VMEM((seq, LRU_GROUP), F32) for _ in range(5)],
        compiler_params=_cparams(("arbitrary", "arbitrary")),
        name="lru",
    )(xa, ga, conv_w, conv_b, wg, lam, h0)


def _lru_gate_weights(w_gates, b_gates):
    depth = w_gates.shape[0]
    heads_per_group = LRU_GROUP // HEAD_DIM
    w = w_gates.astype(BF16).reshape(depth, 2, 2, N_LRU_GROUPS, heads_per_group, HEAD_DIM, HEAD_DIM)
    rows = [jnp.pad(w[:, :, :, :, k], [(0, 0)] * 5 + [(k * HEAD_DIM, LRU_GROUP - (k + 1) * HEAD_DIM)])
            for k in range(heads_per_group)]
    bd = jnp.concatenate(rows, axis=-2)
    wg = jnp.transpose(bd, (0, 3, 4, 1, 2, 5)).reshape(depth, N_LRU_GROUPS, LRU_GROUP, 4 * LRU_GROUP)
    b = 0.5 * b_gates.reshape(depth, 2, 2, N_LRU_GROUPS, LRU_GROUP)
    b = jnp.transpose(b, (0, 3, 1, 2, 4)).reshape(depth, N_LRU_GROUPS, 1, 4 * LRU_GROUP)
    b_hi = b.astype(BF16)
    b_lo = (b - b_hi.astype(F32)).astype(BF16)
    assert N_BIAS_ROWS == 2
    pad = jnp.zeros((depth, N_LRU_GROUPS, LRU_GROUP - N_BIAS_ROWS, 4 * LRU_GROUP), BF16)
    return jnp.concatenate([wg, b_hi, b_lo, pad], axis=2)


def _mixer_output(h, gate, row0, ya, gu_ref, gv_ref, yc, ws_ref, bs_ref, wo_ref):
    tm = h.shape[0]
    half = D_GMLP // 2
    lane = lax.broadcasted_iota(jnp.int32, (GMLP_CHUNK, half), 1)
    first_group = lane < HEAD_DIM
    yb = []
    for c in range(tm // GMLP_CHUNK):
        rows = slice(row0 + c * GMLP_CHUNK, row0 + (c + 1) * GMLP_CHUNK)
        cols = []
        for p in range(2):
            m = jnp.dot(ws_ref[p * 2 * GMLP_CHUNK:(p + 1) * 2 * GMLP_CHUNK, :],
                        gv_ref[rows, p * half:(p + 1) * half], preferred_element_type=F32)
            cols.append(jnp.where(first_group, m[:GMLP_CHUNK], m[GMLP_CHUNK:]))
        mixed = jnp.concatenate(cols, axis=1) + bs_ref[...]
        yb.append((gu_ref[rows, :].astype(F32) * mixed).astype(BF16))
    yb = jnp.concatenate(yb, axis=0)
    y = jnp.dot(ya, wo_ref[0:D_LRU, :], preferred_element_type=F32)
    y = y + jnp.dot(yb, wo_ref[D_LRU:D_LRU + D_GMLP, :], preferred_element_type=F32)
    y = y + jnp.dot(yc, wo_ref[D_LRU + D_GMLP:, :], preferred_element_type=F32)
    return h + gate * y


def _ffn_kernel(*refs, k0, embed, mix, final_norm, parts):
    refs = list(refs)
    take = lambda n: [refs.pop(0) for _ in range(n)]
    (h_ref,) = take(1)
    ctx_ref, tab_ref = take(2) if embed else (None, None)
    (mod_ref,) = take(1)
    ya_ref, yc_ref = take(2) if mix else (None, None)
    yac_ref, ycc_ref = take(2) if mix > 1 else (None, None)
    mix_refs = take(5) if mix else None
    g_ref, wgu_ref, wd_ref = take(3)
    (gf_ref,) = take(1) if final_norm else (None,)
    (o_ref,) = take(1)
    assert not refs

    tile = h_ref.shape[0]
    tm = tile // parts
    shift = mod_ref[k0:k0 + 1, :]
    scale = mod_ref[k0 + 1:k0 + 2, :]
    gate = mod_ref[k0 + 2:k0 + 3, :]
    for part in range(parts):
        row0 = part * tm
        h = h_ref[row0:row0 + tm, :]
        if embed:
            tiles_per_seq, n_latent_tiles = embed
            n_grid_rows = tm // GRID_W
            r0 = (pl.program_id(0) % tiles_per_seq) * (tile // GRID_W) + part * n_grid_rows
            tab = tab_ref[...]
            row_part = jnp.concatenate(
                [jnp.broadcast_to(tab_ref[pl.ds(r0 + a, 1), :], (GRID_W, D_MODEL // 2))
                 for a in range(n_grid_rows)], axis=0)
            col_part = jnp.concatenate([tab] * n_grid_rows, axis=0)
            h = jnp.where(pl.program_id(0) < n_latent_tiles,
                          h + jnp.concatenate([row_part, col_part], axis=1),
                          ctx_ref[row0:row0 + tm, :])
        if mix:
            ya, yc = ya_ref[row0:row0 + tm, :], yc_ref[row0:row0 + tm, :]
            if mix > 1:
                is_latent = pl.program_id(0) < mix - 1
                ya = jnp.where(is_latent, ya, yac_ref[row0:row0 + tm, :])
                yc = jnp.where(is_latent, yc, ycc_ref[row0:row0 + tm, :])
            gu_ref, gv_ref, ws_ref, bs_ref, wo_ref = mix_refs
            h = _mixer_output(h, mod_ref[5:6, :], row0, ya, gu_ref, gv_ref, yc, ws_ref, bs_ref, wo_ref)

        n = _modulated_norm(h, g_ref[...], shift, scale)
        act = []
        for j in range(D_FF // FF_CHUNK):
            lo = j * FF_CHUNK
            gt = jnp.dot(n, wgu_ref[:, lo:lo + FF_CHUNK], preferred_element_type=F32)
            up = jnp.dot(n, wgu_ref[:, D_FF + lo:D_FF + lo + FF_CHUNK], preferred_element_type=F32)
            act.append((gt * jax.nn.sigmoid(gt) * up).astype(BF16))
        acc = jnp.dot(jnp.concatenate(act, axis=1), wd_ref[...], preferred_element_type=F32)
        out = h + 0.5 * gate * acc
        if final_norm:
            out = out * lax.rsqrt(jnp.mean(out * out, axis=-1, keepdims=True) + EPS) * gf_ref[...]
        o_ref[row0:row0 + tm, :] = out


def _ffn(h, mod, norm_g, wgu, wd, *, layer, which, tiles_per_seq, embed=None, mix=None,
         final_g=None, n_tiles=None, tile=ROW_TILE):
    rows = h.shape[0]
    rspec = functools.partial(_row_spec, tile=tile)
    in_specs = [rspec(D_MODEL)]
    args = [h]
    embed_static = 0
    if embed is not None:
        ctx, pos_tab = embed
        n_latent_tiles = rows // tile
        embed_static = (tiles_per_seq, n_latent_tiles)
        rows += ctx.shape[0]
        in_specs = [
            pl.BlockSpec((tile, D_MODEL), lambda i: (jnp.minimum(i, n_latent_tiles - 1), 0)),
            pl.BlockSpec((tile, D_MODEL), lambda i: (jnp.maximum(i - n_latent_tiles, 0), 0)),
            _resident((GRID_W, D_MODEL // 2), lambda i: (0, 0)),
        ]
        args += [ctx, pos_tab]
    n_tiles = rows // tile if n_tiles is None else n_tiles
    in_specs.append(_mod_spec(layer, tiles_per_seq))
    args.append(mod)
    mix_static = 0
    if mix is not None:
        ya, yc, ya_ctx, yc_ctx, gu, gv, ws, bs, w_out = mix
        n_latent = ya.shape[0] // tile
        latent = lambda w: pl.BlockSpec((tile, w), lambda i: (jnp.minimum(i, n_latent - 1), 0))
        context = lambda w: pl.BlockSpec((tile, w), lambda i: (jnp.maximum(i - n_latent, 0), 0))
        in_specs += [latent(D_LRU), latent(D_FNET)]
        args += [ya, yc]
        mix_static = 1
        if ya_ctx is not None:
            in_specs += [context(D_LRU), context(D_FNET)]
            args += [ya_ctx, yc_ctx]
            mix_static = 1 + n_latent
        in_specs += [
            rspec(D_GMLP), rspec(D_GMLP),
            _layer_resident((N_GMLP_GROUPS * GMLP_CHUNK, GMLP_CHUNK), layer),
            _layer_resident((GMLP_CHUNK, D_GMLP), layer),
            _layer_resident((D_MODEL, D_MODEL), layer),
        ]
        args += [gu, gv, ws, bs, w_out]
    in_specs += [
        _layer_resident((1, D_MODEL), layer, 2 * which),
        _layer_resident((D_MODEL, 2 * D_FF), layer, which),
        _layer_resident((D_FF, D_MODEL), layer, which),
    ]
    args += [norm_g, wgu, wd]
    if final_g is not None:
        in_specs.append(_resident((1, D_MODEL), lambda i: (0, 0)))
        args.append(final_g.reshape(1, D_MODEL))
    kern = functools.partial(_ffn_kernel, k0=6 * which, embed=embed_static,
                             mix=mix_static, final_norm=final_g is not None,
                             parts=tile // FFN_PART_ROWS)
    return pl.pallas_call(
        kern,
        grid=(n_tiles,),
        in_specs=in_specs,
        out_specs=rspec(D_MODEL),
        out_shape=jax.ShapeDtypeStruct((n_tiles * tile, D_MODEL), F32),
        compiler_params=_cparams(("arbitrary",)),
        name="ffn",
    )(*args)


def _mix_in_kernel(h_ref, mod_ref, g_ref, w_ref, xa_ref, ga_ref, gu_ref, gv_ref, f_ref, fc_ref,
                   fs_ref, *, radix, n_latent_tiles):
    tile = h_ref.shape[0]
    n_slabs = D_FNET // LANES
    for part in range(MIX_IN_PARTS):
        rows = slice(part * tile // MIX_IN_PARTS, (part + 1) * tile // MIX_IN_PARTS)
        n = _modulated_norm(h_ref[rows, :], g_ref[...], mod_ref[3:4, :], mod_ref[4:5, :])

        def proj(lo, width):
            return jnp.dot(n, w_ref[:, lo:lo + width], preferred_element_type=F32)

        xa_ref[rows, :] = proj(0, D_LRU)
        ga_ref[rows, :] = jax.nn.gelu(proj(D_LRU, D_LRU)).astype(BF16)
        gu_ref[rows, :] = jax.nn.gelu(proj(2 * D_LRU, D_GMLP)).astype(BF16)
        gv_ref[rows, :] = jax.nn.gelu(proj(2 * D_LRU + D_GMLP, D_GMLP)).astype(BF16)
        f = proj(2 * D_LRU + 2 * D_GMLP, D_FNET)
        for s in range(n_slabs):
            fs_ref[s, rows, :] = f[:, s * LANES:(s + 1) * LANES]

    is_latent = pl.program_id(0) < n_latent_tiles

    @pl.when(is_latent)
    def _():
        per_class = tile // radix
        for a in range(radix):
            piece = [fs_ref[s, pl.ds(a, per_class, stride=radix), :] for s in range(n_slabs)]
            f_ref[a] = jnp.concatenate(piece, axis=1).astype(BF16)

    @pl.when(jnp.logical_not(is_latent))
    def _():
        fc_ref[...] = jnp.concatenate([fs_ref[s] for s in range(n_slabs)], axis=1).astype(BF16)


def _mix_in(h, mod, norm_g, w_in, *, layer, radix, tiles_per_seq, n_latent_tiles):
    rows = h.shape[0]
    tile = MIX_IN_TILE
    n_tiles = rows // tile
    rspec = functools.partial(_row_spec, tile=tile)
    per_class = tile // radix
    latent = lambda i: jnp.minimum(i, n_latent_tiles - 1)
    return pl.pallas_call(
        functools.partial(_mix_in_kernel, radix=radix, n_latent_tiles=n_latent_tiles),
        grid=(n_tiles,),
        in_specs=[
            rspec(D_MODEL),
            _mod_spec(layer, tiles_per_seq),
            _layer_resident((1, D_MODEL), layer, 1),
            _layer_resident((D_MODEL, D_IN), layer),
        ],
        out_specs=[
            rspec(D_LRU), rspec(D_LRU), rspec(D_GMLP), rspec(D_GMLP),
            pl.BlockSpec((None, radix, per_class, D_FNET),
                         lambda i: (latent(i) // tiles_per_seq, 0, latent(i) % tiles_per_seq, 0)),
            pl.BlockSpec((tile, D_FNET), lambda i: (jnp.maximum(i - n_latent_tiles, 0), 0)),
        ],
        out_shape=[
            jax.ShapeDtypeStruct((rows, D_LRU), F32),
            jax.ShapeDtypeStruct((rows, D_LRU), BF16),
            jax.ShapeDtypeStruct((rows, D_GMLP), BF16),
            jax.ShapeDtypeStruct((rows, D_GMLP), BF16),
            jax.ShapeDtypeStruct((n_latent_tiles // tiles_per_seq, radix, tiles_per_seq * per_class,
                                  D_FNET), BF16),
            jax.ShapeDtypeStruct(((n_tiles - n_latent_tiles) * tile, D_FNET), BF16),
        ],
        scratch_shapes=[pltpu.VMEM((D_FNET // LANES, tile, LANES), F32)],
        compiler_params=_cparams(("arbitrary",)),
        name="mix_in",
    )(h, mod, norm_g, w_in)


def _fft(xs):
    n = len(xs)
    if n == 1:
        return xs
    even, odd = _fft(xs[0::2]), _fft(xs[1::2])
    out = [None] * n
    for k in range(n // 2):
        (er, ei), (qr, qi) = even[k], odd[k]
        if k == 0:
            tr, ti = qr, qi
        elif 4 * k == n:
            tr, ti = qi, -qr
        else:
            wr, wi = math.cos(2 * math.pi * k / n), -math.sin(2 * math.pi * k / n)
            tr, ti = wr * qr - wi * qi, wr * qi + wi * qr
        out[k] = (er + tr, ei + ti)
        out[k + n // 2] = (er - tr, ei - ti)
    return out


def _fourier_kernel(x_ref, t_ref, cs_ref, o_ref, u_ref, p_ref, *, radix):
    q = DFT_Q
    n_half = D_FNET // LANES
    for a in range(radix):
        u = jnp.dot(t_ref[a], x_ref[a], preferred_element_type=F32)
        for part in range(2):
            for s in range(n_half):
                dst = p_ref if radix == 1 else u_ref.at[a]
                dst[part * n_half + s] = u[part * q:(part + 1) * q, s * LANES:(s + 1) * LANES]

    if radix > 1:
        def butterflies(i, _):
            s = i // (q // SUBLANES)
            r = pl.multiple_of((i % (q // SUBLANES)) * SUBLANES, SUBLANES)
            xs = [(u_ref[a, s, pl.ds(r, SUBLANES), :], u_ref[a, n_half + s, pl.ds(r, SUBLANES), :])
                  for a in range(radix)]
            for c, (pr, pi) in enumerate(_fft(xs)):
                p_ref[s, pl.ds(c * q + r, SUBLANES), :] = pr
                p_ref[n_half + s, pl.ds(c * q + r, SUBLANES), :] = pi
            return 0

        lax.fori_loop(0, n_half * (q // SUBLANES), butterflies, 0, unroll=2)

    for c in range(radix):
        rows = slice(c * q, (c + 1) * q)
        p = jnp.concatenate([p_ref[s, rows, :] for s in range(2 * n_half)], axis=1).astype(BF16)
        o_ref[rows, :] = jnp.dot(p, cs_ref[...], preferred_element_type=F32).astype(BF16)


def _fourier(f, tables, cs, *, batch, seq):
    radix = seq // DFT_Q
    n_slab = 2 * (D_FNET // LANES)
    return pl.pallas_call(
        functools.partial(_fourier_kernel, radix=radix),
        grid=(batch,),
        in_specs=[
            pl.BlockSpec((None, radix, DFT_Q, D_FNET), lambda b: (b, 0, 0, 0)),
            _resident((radix, 2 * DFT_Q, DFT_Q), lambda b: (0, 0, 0)),
            _resident((2 * D_FNET, D_FNET), lambda b: (0, 0)),
        ],
        out_specs=pl.BlockSpec((seq, D_FNET), lambda b: (b, 0)),
        out_shape=jax.ShapeDtypeStruct((batch * seq, D_FNET), BF16),
        scratch_shapes=[pltpu.VMEM((radix, n_slab, DFT_Q, LANES), F32),
                        pltpu.VMEM((n_slab, seq, LANES), F32)],
        compiler_params=_cparams(("arbitrary",)),
        name="fourier",
    )(f.reshape(batch, radix, DFT_Q, D_FNET), tables, cs)


def _position_dft_tables(seq):
    radix = seq // DFT_Q
    scale = 1.0 / math.sqrt(seq * HEAD_DIM)
    a = jnp.arange(radix, dtype=jnp.int32)[:, None]
    d = jnp.arange(DFT_Q, dtype=jnp.int32)
    ang_a = ((d[None, :] * a) % seq).astype(F32) * (2.0 * math.pi / seq)
    ang_b = ((d[:, None] * d[None, :]) % DFT_Q).astype(F32) * (2.0 * math.pi / DFT_Q)
    ca, sa = (jnp.cos(ang_a) * scale)[:, :, None], (jnp.sin(ang_a) * scale)[:, :, None]
    cb, sb = jnp.cos(ang_b)[None], jnp.sin(ang_b)[None]
    cos = ca * cb - sa * sb
    msin = -(sa * cb + ca * sb)
    return jnp.concatenate([cos, msin], axis=1).astype(BF16)


def _channel_dft_matrix():
    c = jnp.arange(D_FNET, dtype=jnp.int32)
    same_group = (c[:, None] // HEAD_DIM) == (c[None, :] // HEAD_DIM)
    ang = (((c[:, None] % HEAD_DIM) * (c[None, :] % HEAD_DIM)) % HEAD_DIM).astype(F32) * (
        2.0 * math.pi / HEAD_DIM)
    cs = jnp.concatenate([jnp.where(same_group, jnp.cos(ang), 0.0),
                          jnp.where(same_group, jnp.sin(ang), 0.0)], axis=0)
    return cs.astype(BF16)


def _pos_table_kernel(f_ref, o_ref):
    j = lax.broadcasted_iota(jnp.int32, (GRID_W, D_MODEL // 4), 0).astype(F32)
    e = j * f_ref[...]
    o_ref[...] = jnp.concatenate([jnp.sin(e), jnp.cos(e)], axis=1)


def _pos_table():
    q = D_MODEL // 4
    freqs = 1.0 / (10000.0 ** (jnp.arange(q, dtype=F32) / q))
    return pl.pallas_call(
        _pos_table_kernel,
        out_shape=jax.ShapeDtypeStruct((GRID_W, D_MODEL // 2), F32),
        name="pos_table",
    )(freqs.reshape(1, q))


def kernel(x, c, ctx, c_ctx, w_mod, b_mod, norm_g, ffn_w_gu, ffn_w_down, w_in, w_out, conv_w, conv_b,
           lru_w_gates, lru_b_gates, lru_lambda, gmlp_ws, gmlp_bs, final_norm_g):
    batch, seq, _ = x.shape
    ctx_len = ctx.shape[1]
    depth = w_mod.shape[0]
    assert batch < MOD_ROWS and seq % ROW_TILE == 0 and (batch * ctx_len) % ROW_TILE == 0
    assert seq % (N_SEG * LRU_CHUNK) == 0 and ctx_len % (N_SEG * SUBLANES) == 0
    assert LRU_CHUNK % (ctx_len // N_SEG) == 0 or (ctx_len // N_SEG) % LRU_CHUNK == 0
    radix = seq // DFT_Q
    assert seq == radix * DFT_Q and radix & (radix - 1) == 0 and ROW_TILE % radix == 0
    assert ctx_len == DFT_Q
    assert seq % MIX_IN_TILE == 0 and (batch * ctx_len) % MIX_IN_TILE == 0 and MIX_IN_TILE % radix == 0
    ctx_row = batch
    latent_rows = batch * seq
    tiles = dict(tiles_per_seq=seq // ROW_TILE)

    cvec = jnp.zeros((MOD_ROWS, D_MODEL), F32).at[:batch].set(c).at[ctx_row].set(c_ctx)
    mod = _adaln(cvec, w_mod, b_mod).reshape(depth, MOD_ROWS, N_MOD, D_MODEL)

    wgu = ffn_w_gu.astype(BF16)
    wd = ffn_w_down.astype(BF16)
    w_in_b = w_in.astype(BF16)
    w_out_b = w_out.astype(BF16)
    norm_g4 = norm_g.reshape(depth, 3, 1, D_MODEL)
    ws = gmlp_ws.astype(BF16).reshape(depth, N_GMLP_GROUPS * GMLP_CHUNK, GMLP_CHUNK)
    bs = jnp.repeat(jnp.swapaxes(gmlp_bs, 1, 2), HEAD_DIM, axis=2)
    wg = _lru_gate_weights(lru_w_gates, lru_b_gates)
    conv_b3 = conv_b.reshape(depth, 1, D_LRU)
    cs = _channel_dft_matrix()
    tab_lat = _position_dft_tables(seq)
    tab_ctx = _position_dft_tables(ctx_len)
    pos_tab = _pos_table()

    h = x.reshape(latent_rows, D_MODEL)
    zero_state = jnp.zeros((batch, 2, D_LRU), F32)

    for l in range(depth):
        last = l == depth - 1
        ffn = functools.partial(_ffn, mod=mod, norm_g=norm_g4, wgu=wgu, wd=wd, layer=l, **tiles)
        lru = functools.partial(_lru, conv_w=conv_w, conv_b=conv_b3, wg=wg, lam=lru_lambda,
                                layer=l, batch=batch)

        h = ffn(h, which=0,
                embed=(ctx.reshape(batch * ctx_len, D_MODEL), pos_tab) if l == 0 else None)
        xa, ga, gu, gv, f, f_c = _mix_in(h, mod, norm_g4, w_in_b, layer=l, radix=radix,
                                         tiles_per_seq=seq // MIX_IN_TILE,
                                         n_latent_tiles=latent_rows // MIX_IN_TILE)
        ya_c, state_c = lru(xa, ga, h0=zero_state, seq=ctx_len, first_row=latent_rows)
        ya, _ = lru(xa, ga, h0=state_c, seq=seq)
        yc = _fourier(f, tab_lat, cs, batch=batch, seq=seq)
        if last:
            ya_c = yc_c = None
        else:
            yc_c = _fourier(f_c, tab_ctx, cs, batch=batch, seq=ctx_len)
        h = ffn(h, which=1, mix=(ya, yc, ya_c, yc_c, gu, gv, ws, bs, w_out_b),
                final_g=final_norm_g if last else None,
                n_tiles=latent_rows // ROW_TILE if last else None)
    return h.reshape(batch, seq, D_MODEL)
```

```python
import functools
import math

import jax
import jax.numpy as jnp
from jax import lax
from jax.experimental import pallas as pl
from jax.experimental.pallas import tpu as pltpu

D_MODEL = 1024
GRID_W = 64
HEAD_DIM = 64
D_FF = ((8 * D_MODEL // 3 + 127) // 128) * 128
D_LRU = D_MODEL // 2
N_LRU_HEADS = D_LRU // HEAD_DIM
D_GMLP = D_MODEL // 4
N_GMLP_GROUPS = D_GMLP // HEAD_DIM
GMLP_CHUNK = 128
D_FNET = D_MODEL // 4
N_FNET_GROUPS = D_FNET // HEAD_DIM
D_IN = 2 * D_LRU + 2 * D_GMLP + D_FNET
CONV_W = 4
LRU_C = 8.0
N_MOD = 9
EPS = 1e-6

LANES = 128
SUBLANES = 8
VMEM_LIMIT_BYTES = 56 * 1024 * 1024

ROW_TILE = 1024
FFN_PART_ROWS = 512
MIX_IN_TILE = 1024
MIX_IN_PARTS = 4
FF_CHUNK = 256
MOD_ROWS = 16
LRU_GROUP = LANES
N_LRU_GROUPS = D_LRU // LRU_GROUP
N_SEG = SUBLANES
LRU_CHUNK = 64
ADALN_COLS = 3 * D_MODEL
LRU_STEP_ROWS = 1024
N_BIAS_ROWS = 2
DFT_Q = 256

BF16 = jnp.bfloat16
F32 = jnp.float32
F32_TINY = float(jnp.finfo(jnp.float32).tiny)
LOG2_E = math.log2(math.e)


def _cparams(sem):
    return pltpu.CompilerParams(dimension_semantics=sem, vmem_limit_bytes=VMEM_LIMIT_BYTES)


def _resident(shape, index_map):
    return pl.BlockSpec(shape, index_map, pipeline_mode=pl.Buffered(1))


def _adaln_kernel(c_ref, w_ref, b_ref, o_ref):
    c = c_ref[...]
    s = (c * jax.nn.sigmoid(c)).astype(BF16)
    o_ref[...] = jnp.dot(s, w_ref[...].astype(BF16), preferred_element_type=F32) + b_ref[...]


def _adaln(cvec, w_mod, b_mod):
    depth = w_mod.shape[0]
    return pl.pallas_call(
        _adaln_kernel,
        grid=(depth, N_MOD * D_MODEL // ADALN_COLS),
        in_specs=[
            pl.BlockSpec((MOD_ROWS, D_MODEL), lambda l, j: (0, 0)),
            pl.BlockSpec((None, D_MODEL, ADALN_COLS), lambda l, j: (l, 0, j)),
            pl.BlockSpec((None, 1, ADALN_COLS), lambda l, j: (l, 0, j)),
        ],
        out_specs=pl.BlockSpec((None, MOD_ROWS, ADALN_COLS), lambda l, j: (l, 0, j)),
        out_shape=jax.ShapeDtypeStruct((depth, MOD_ROWS, N_MOD * D_MODEL), F32),
        compiler_params=_cparams(("arbitrary", "arbitrary")),
        name="adaln",
    )(cvec, w_mod, b_mod.reshape(depth, 1, N_MOD * D_MODEL))


def _modulated_norm(h, g, shift, scale):
    y = h * lax.rsqrt(jnp.mean(h * h, axis=-1, keepdims=True) + EPS)
    return ((y * g) * (1.0 + scale) + shift).astype(BF16)


def _mod_spec(layer, tiles_per_seq):
    return pl.BlockSpec((None, None, N_MOD, D_MODEL), lambda i: (layer, i // tiles_per_seq, 0, 0))


def _row_spec(width, tile=ROW_TILE):
    return pl.BlockSpec((tile, width), lambda i: (i, 0))


def _layer_resident(shape, *lead):
    zeros = (0,) * len(shape)
    return _resident((None,) * len(lead) + shape, lambda i: lead + zeros)


def _lru_coefficients(g, xh, c_half):
    t_r = jnp.tanh(g[:, :LRU_GROUP])
    t_i = jnp.tanh(g[:, LRU_GROUP:])
    m = c_half * t_r + c_half
    a = jnp.exp2(m * (-LOG2_E))
    z = jnp.tanh(m) * (a * a + 1.0)
    root = z * lax.rsqrt(jnp.maximum(z, F32_TINY))
    b = (root * xh) * (t_i + 1.0)
    return a, b


def _lru_kernel(xa_ref, ga_ref, cw_ref, cb_ref, wg_ref, lam_ref, h0_ref,
                ya_ref, st_ref, xin_ref, xc_ref, pf_ref, hf_ref, pb_ref, hb_ref, *, seq):
    seg = seq // N_SEG
    chunk = min(LRU_CHUNK, seg)
    blk = chunk * N_SEG
    n_blk = seg // chunk
    v = N_SEG

    for j in range(N_SEG):
        xin_ref[pl.ds(2 * v + j, seg, stride=N_SEG), :] = xa_ref[j * seg:(j + 1) * seg, :]
    zrow = jnp.zeros((1, LRU_GROUP), F32)
    for k in (0, 1):
        tail = xin_ref[(seg + k) * v:(seg + k + 1) * v, :]
        xin_ref[k * v:(k + 1) * v, :] = jnp.concatenate([zrow, tail[:N_SEG - 1]], axis=0)
    head = xin_ref[2 * v:3 * v, :]
    xin_ref[(seg + 2) * v:(seg + 3) * v, :] = jnp.concatenate([head[1:], zrow], axis=0)

    cw = 0.5 * cw_ref[...]
    cb = 0.5 * cb_ref[...]
    for c in range(n_blk):
        acc = cb
        for k in range(CONV_W):
            acc = acc + xin_ref[c * blk + k * v:c * blk + k * v + blk, :] * cw[k:k + 1, :]
        xc_ref[c * blk:(c + 1) * blk, :] = acc

    c_half = (0.5 * LRU_C) * jax.nn.softplus(-lam_ref[...])
    lane = lax.broadcasted_iota(jnp.int32, (blk, LRU_GROUP), 1)
    ones = jnp.where(lane < N_BIAS_ROWS, 1.0, 0.0).astype(BF16)

    def block(c, d, state, p_ref, h_ref):
        rows = slice(c * blk, (c + 1) * blk)
        xh = xc_ref[rows, :]
        cols = slice(2 * d * LRU_GROUP, (2 * d + 2) * LRU_GROUP)
        g = jnp.dot(jnp.concatenate([xh.astype(BF16), ones], axis=1), wg_ref[:, cols],
                    preferred_element_type=F32)
        a, b = _lru_coefficients(g, xh, c_half[d:d + 1, :])
        p, h = state
        ps, hs = [], []
        steps = range(chunk) if d == 0 else reversed(range(chunk))
        for s in steps:
            a_s = a[s * v:(s + 1) * v, :]
            p = a_s * p
            h = a_s * h + b[s * v:(s + 1) * v, :]
            ps.append(p)
            hs.append(h)
        if d == 1:
            ps, hs = ps[::-1], hs[::-1]
        p_ref[rows, :] = jnp.concatenate(ps, axis=0)
        h_ref[rows, :] = jnp.concatenate(hs, axis=0)
        return p, h

    one = jnp.ones((N_SEG, LRU_GROUP), F32)
    zero = jnp.zeros((N_SEG, LRU_GROUP), F32)
    fwd = bwd = (one, zero)
    for c in range(n_blk):
        fwd = block(c, 0, fwd, pf_ref, hf_ref)
        bwd = block(n_blk - 1 - c, 1, bwd, pb_ref, hb_ref)
    (pf, hf), (pb, hb) = fwd, bwd

    h0 = h0_ref[...]
    c = h0[0:1, :]
    rows = []
    for j in range(N_SEG):
        rows.append(c)
        c = hf[j:j + 1, :] + pf[j:j + 1, :] * c
    carry_f = jnp.concatenate(rows * chunk, axis=0)
    final_f = c
    c = h0[1:2, :]
    rows = []
    for j in reversed(range(N_SEG)):
        rows.append(c)
        c = hb[j:j + 1, :] + pb[j:j + 1, :] * c
    carry_b = jnp.concatenate(rows[::-1] * chunk, axis=0)
    final_b = c
    st_ref[...] = jnp.concatenate([final_f, final_b], axis=0)

    for c in range(n_blk):
        rows = slice(c * blk, (c + 1) * blk)
        xin_ref[rows, :] = ((hf_ref[rows, :] + pf_ref[rows, :] * carry_f)
                            + (hb_ref[rows, :] + pb_ref[rows, :] * carry_b))
    for j in range(N_SEG):
        y = xin_ref[pl.ds(j, seg, stride=N_SEG), :]
        ga = ga_ref[j * seg:(j + 1) * seg, :].astype(F32)
        ya_ref[j * seg:(j + 1) * seg, :] = (y * ga).astype(BF16)


def _lru_groups_kernel(xa_ref, ga_ref, cw_ref, cb_ref, wg_ref, lam_ref, h0_ref,
                       ya_ref, st_ref, *scratch, seq, groups):
    for k in range(groups):
        cols = slice(k * LRU_GROUP, (k + 1) * LRU_GROUP)
        _lru_kernel(xa_ref.at[:, cols], ga_ref.at[:, cols], cw_ref.at[:, cols], cb_ref.at[:, cols],
                    wg_ref.at[k], lam_ref.at[:, cols], h0_ref.at[:, cols],
                    ya_ref.at[:, cols], st_ref.at[:, cols], *scratch, seq=seq)


def _lru(xa, ga, conv_w, conv_b, wg, lam, h0, *, layer, batch, seq, first_row=0):
    seg = seq // N_SEG
    first_block = first_row // seq
    assert first_row == first_block * seq
    groups = min(N_LRU_GROUPS, max(1, LRU_STEP_ROWS // seq))
    assert N_LRU_GROUPS % groups == 0
    width = groups * LRU_GROUP
    src = lambda: pl.BlockSpec((seq, width), lambda b, g: (first_block + b, g))
    grp = lambda r: pl.BlockSpec((None, r, width), lambda b, g: (layer, 0, g))
    st = lambda: pl.BlockSpec((None, 2, width), lambda b, g: (b, 0, g))
    return pl.pallas_call(
        functools.partial(_lru_groups_kernel, seq=seq, groups=groups),
        grid=(batch, N_LRU_GROUPS // groups),
        in_specs=[
            src(), src(), grp(CONV_W), grp(1),
            pl.BlockSpec((None, groups, 2 * LRU_GROUP, 4 * LRU_GROUP),
                         lambda b, g: (layer, g, 0, 0)),
            grp(2), st(),
        ],
        out_specs=[pl.BlockSpec((seq, width), lambda b, g: (b, g)), st()],
        out_shape=[
            jax.ShapeDtypeStruct((batch * seq, D_LRU), BF16),
            jax.ShapeDtypeStruct((batch, 2, D_LRU), F32),
        ],
        scratch_shapes=[pltpu.VMEM(((seg + CONV_W - 1) * N_SEG, LRU_GROUP), F32)]
        + [pltpu.VMEM((seq, LRU_GROUP), F32) for _ in range(5)],
        compiler_params=_cparams(("arbitrary", "arbitrary")),
        name="lru",
    )(xa, ga, conv_w, conv_b, wg, lam, h0)


def _lru_gate_weights(w_gates, b_gates):
    depth = w_gates.shape[0]
    heads_per_group = LRU_GROUP // HEAD_DIM
    w = w_gates.astype(BF16).reshape(depth, 2, 2, N_LRU_GROUPS, heads_per_group, HEAD_DIM, HEAD_DIM)
    rows = [jnp.pad(w[:, :, :, :, k], [(0, 0)] * 5 + [(k * HEAD_DIM, LRU_GROUP - (k + 1) * HEAD_DIM)])
            for k in range(heads_per_group)]
    bd = jnp.concatenate(rows, axis=-2)
    wg = jnp.transpose(bd, (0, 3, 4, 1, 2, 5)).reshape(depth, N_LRU_GROUPS, LRU_GROUP, 4 * LRU_GROUP)
    b = 0.5 * b_gates.reshape(depth, 2, 2, N_LRU_GROUPS, LRU_GROUP)
    b = jnp.transpose(b, (0, 3, 1, 2, 4)).reshape(depth, N_LRU_GROUPS, 1, 4 * LRU_GROUP)
    b_hi = b.astype(BF16)
    b_lo = (b - b_hi.astype(F32)).astype(BF16)
    assert N_BIAS_ROWS == 2
    pad = jnp.zeros((depth, N_LRU_GROUPS, LRU_GROUP - N_BIAS_ROWS, 4 * LRU_GROUP), BF16)
    return jnp.concatenate([wg, b_hi, b_lo, pad], axis=2)


def _mixer_output(h, gate, row0, ya, gu_ref, gv_ref, yc, ws_ref, bs_ref, wo_ref):
    tm = h.shape[0]
    half = D_GMLP // 2
    lane = lax.broadcasted_iota(jnp.int32, (GMLP_CHUNK, half), 1)
    first_group = lane < HEAD_DIM
    yb = []
    for c in range(tm // GMLP_CHUNK):
        rows = slice(row0 + c * GMLP_CHUNK, row0 + (c + 1) * GMLP_CHUNK)
        cols = []
        for p in range(2):
            m = jnp.dot(ws_ref[p * 2 * GMLP_CHUNK:(p + 1) * 2 * GMLP_CHUNK, :],
                        gv_ref[rows, p * half:(p + 1) * half], preferred_element_type=F32)
            cols.append(jnp.where(first_group, m[:GMLP_CHUNK], m[GMLP_CHUNK:]))
        mixed = jnp.concatenate(cols, axis=1) + bs_ref[...]
        yb.append((gu_ref[rows, :].astype(F32) * mixed).astype(BF16))
    yb = jnp.concatenate(yb, axis=0)
    y = jnp.dot(ya, wo_ref[0:D_LRU, :], preferred_element_type=F32)
    y = y + jnp.dot(yb, wo_ref[D_LRU:D_LRU + D_GMLP, :], preferred_element_type=F32)
    y = y + jnp.dot(yc, wo_ref[D_LRU + D_GMLP:, :], preferred_element_type=F32)
    return h + gate * y


def _ffn_kernel(*refs, k0, embed, mix, final_norm, parts):
    refs = list(refs)
    take = lambda n: [refs.pop(0) for _ in range(n)]
    (h_ref,) = take(1)
    ctx_ref, tab_ref = take(2) if embed else (None, None)
    (mod_ref,) = take(1)
    ya_ref, yc_ref = take(2) if mix else (None, None)
    yac_ref, ycc_ref = take(2) if mix > 1 else (None, None)
    mix_refs = take(5) if mix else None
    g_ref, wgu_ref, wd_ref = take(3)
    (gf_ref,) = take(1) if final_norm else (None,)
    (o_ref,) = take(1)
    assert not refs

    tile = h_ref.shape[0]
    tm = tile // parts
    shift = mod_ref[k0:k0 + 1, :]
    scale = mod_ref[k0 + 1:k0 + 2, :]
    gate = mod_ref[k0 + 2:k0 + 3, :]
    for part in range(parts):
        row0 = part * tm
        h = h_ref[row0:row0 + tm, :]
        if embed:
            tiles_per_seq, n_latent_tiles = embed
            n_grid_rows = tm // GRID_W
            r0 = (pl.program_id(0) % tiles_per_seq) * (tile // GRID_W) + part * n_grid_rows
            tab = tab_ref[...]
            row_part = jnp.concatenate(
                [jnp.broadcast_to(tab_ref[pl.ds(r0 + a, 1), :], (GRID_W, D_MODEL // 2))
                 for a in range(n_grid_rows)], axis=0)
            col_part = jnp.concatenate([tab] * n_grid_rows, axis=0)
            h = jnp.where(pl.program_id(0) < n_latent_tiles,
                          h + jnp.concatenate([row_part, col_part], axis=1),
                          ctx_ref[row0:row0 + tm, :])
        if mix:
            ya, yc = ya_ref[row0:row0 + tm, :], yc_ref[row0:row0 + tm, :]
            if mix > 1:
                is_latent = pl.program_id(0) < mix - 1
                ya = jnp.where(is_latent, ya, yac_ref[row0:row0 + tm, :])
                yc = jnp.where(is_latent, yc, ycc_ref[row0:row0 + tm, :])
            gu_ref, gv_ref, ws_ref, bs_ref, wo_ref = mix_refs
            h = _mixer_output(h, mod_ref[5:6, :], row0, ya, gu_ref, gv_ref, yc, ws_ref, bs_ref, wo_ref)

        n = _modulated_norm(h, g_ref[...], shift, scale)
        act = []
        for j in range(D_FF // FF_CHUNK):
            lo = j * FF_CHUNK
            gt = jnp.dot(n, wgu_ref[:, lo:lo + FF_CHUNK], preferred_element_type=F32)
            up = jnp.dot(n, wgu_ref[:, D_FF + lo:D_FF + lo + FF_CHUNK], preferred_element_type=F32)
            act.append((gt * jax.nn.sigmoid(gt) * up).astype(BF16))
        acc = jnp.dot(jnp.concatenate(act, axis=1), wd_ref[...], preferred_element_type=F32)
        out = h + 0.5 * gate * acc
        if final_norm:
            out = out * lax.rsqrt(jnp.mean(out * out, axis=-1, keepdims=True) + EPS) * gf_ref[...]
        o_ref[row0:row0 + tm, :] = out


def _ffn(h, mod, norm_g, wgu, wd, *, layer, which, tiles_per_seq, embed=None, mix=None,
         final_g=None, n_tiles=None, tile=ROW_TILE):
    rows = h.shape[0]
    rspec = functools.partial(_row_spec, tile=tile)
    in_specs = [rspec(D_MODEL)]
    args = [h]
    embed_static = 0
    if embed is not None:
        ctx, pos_tab = embed
        n_latent_tiles = rows // tile
        embed_static = (tiles_per_seq, n_latent_tiles)
        rows += ctx.shape[0]
        in_specs = [
            pl.BlockSpec((tile, D_MODEL), lambda i: (jnp.minimum(i, n_latent_tiles - 1), 0)),
            pl.BlockSpec((tile, D_MODEL), lambda i: (jnp.maximum(i - n_latent_tiles, 0), 0)),
            _resident((GRID_W, D_MODEL // 2), lambda i: (0, 0)),
        ]
        args += [ctx, pos_tab]
    n_tiles = rows // tile if n_tiles is None else n_tiles
    in_specs.append(_mod_spec(layer, tiles_per_seq))
    args.append(mod)
    mix_static = 0
    if mix is not None:
        ya, yc, ya_ctx, yc_ctx, gu, gv, ws, bs, w_out = mix
        n_latent = ya.shape[0] // tile
        latent = lambda w: pl.BlockSpec((tile, w), lambda i: (jnp.minimum(i, n_latent - 1), 0))
        context = lambda w: pl.BlockSpec((tile, w), lambda i: (jnp.maximum(i - n_latent, 0), 0))
        in_specs += [latent(D_LRU), latent(D_FNET)]
        args += [ya, yc]
        mix_static = 1
        if ya_ctx is not None:
            in_specs += [context(D_LRU), context(D_FNET)]
            args += [ya_ctx, yc_ctx]
            mix_static = 1 + n_latent
        in_specs += [
            rspec(D_GMLP), rspec(D_GMLP),
            _layer_resident((N_GMLP_GROUPS * GMLP_CHUNK, GMLP_CHUNK), layer),
            _layer_resident((GMLP_CHUNK, D_GMLP), layer),
            _layer_resident((D_MODEL, D_MODEL), layer),
        ]
        args += [gu, gv, ws, bs, w_out]
    in_specs += [
        _layer_resident((1, D_MODEL), layer, 2 * which),
        _layer_resident((D_MODEL, 2 * D_FF), layer, which),
        _layer_resident((D_FF, D_MODEL), layer, which),
    ]
    args += [norm_g, wgu, wd]
    if final_g is not None:
        in_specs.append(_resident((1, D_MODEL), lambda i: (0, 0)))
        args.append(final_g.reshape(1, D_MODEL))
    kern = functools.partial(_ffn_kernel, k0=6 * which, embed=embed_static,
                             mix=mix_static, final_norm=final_g is not None,
                             parts=tile // FFN_PART_ROWS)
    return pl.pallas_call(
        kern,
        grid=(n_tiles,),
        in_specs=in_specs,
        out_specs=rspec(D_MODEL),
        out_shape=jax.ShapeDtypeStruct((n_tiles * tile, D_MODEL), F32),
        compiler_params=_cparams(("arbitrary",)),
        name="ffn",
    )(*args)


def _mix_in_kernel(h_ref, mod_ref, g_ref, w_ref, xa_ref, ga_ref, gu_ref, gv_ref, f_ref, fc_ref,
                   fs_ref, *, radix, n_latent_tiles):
    tile = h_ref.shape[0]
    n_slabs = D_FNET // LANES
    for part in range(MIX_IN_PARTS):
        rows = slice(part * tile // MIX_IN_PARTS, (part + 1) * tile // MIX_IN_PARTS)
        n = _modulated_norm(h_ref[rows, :], g_ref[...], mod_ref[3:4, :], mod_ref[4:5, :])

        def proj(lo, width):
            return jnp.dot(n, w_ref[:, lo:lo + width], preferred_element_type=F32)

        xa_ref[rows, :] = proj(0, D_LRU)
        ga_ref[rows, :] = jax.nn.gelu(proj(D_LRU, D_LRU)).astype(BF16)
        gu_ref[rows, :] = jax.nn.gelu(proj(2 * D_LRU, D_GMLP)).astype(BF16)
        gv_ref[rows, :] = jax.nn.gelu(proj(2 * D_LRU + D_GMLP, D_GMLP)).astype(BF16)
        f = proj(2 * D_LRU + 2 * D_GMLP, D_FNET)
        for s in range(n_slabs):
            fs_ref[s, rows, :] = f[:, s * LANES:(s + 1) * LANES]

    is_latent = pl.program_id(0) < n_latent_tiles

    @pl.when(is_latent)
    def _():
        per_class = tile // radix
        for a in range(radix):
            piece = [fs_ref[s, pl.ds(a, per_class, stride=radix), :] for s in range(n_slabs)]
            f_ref[a] = jnp.concatenate(piece, axis=1).astype(BF16)

    @pl.when(jnp.logical_not(is_latent))
    def _():
        fc_ref[...] = jnp.concatenate([fs_ref[s] for s in range(n_slabs)], axis=1).astype(BF16)


def _mix_in(h, mod, norm_g, w_in, *, layer, radix, tiles_per_seq, n_latent_tiles):
    rows = h.shape[0]
    tile = MIX_IN_TILE
    n_tiles = rows // tile
    rspec = functools.partial(_row_spec, tile=tile)
    per_class = tile // radix
    latent = lambda i: jnp.minimum(i, n_latent_tiles - 1)
    return pl.pallas_call(
        functools.partial(_mix_in_kernel, radix=radix, n_latent_tiles=n_latent_tiles),
        grid=(n_tiles,),
        in_specs=[
            rspec(D_MODEL),
            _mod_spec(layer, tiles_per_seq),
            _layer_resident((1, D_MODEL), layer, 1),
            _layer_resident((D_MODEL, D_IN), layer),
        ],
        out_specs=[
            rspec(D_LRU), rspec(D_LRU), rspec(D_GMLP), rspec(D_GMLP),
            pl.BlockSpec((None, radix, per_class, D_FNET),
                         lambda i: (latent(i) // tiles_per_seq, 0, latent(i) % tiles_per_seq, 0)),
            pl.BlockSpec((tile, D_FNET), lambda i: (jnp.maximum(i - n_latent_tiles, 0), 0)),
        ],
        out_shape=[
            jax.ShapeDtypeStruct((rows, D_LRU), F32),
            jax.ShapeDtypeStruct((rows, D_LRU), BF16),
            jax.ShapeDtypeStruct((rows, D_GMLP), BF16),
            jax.ShapeDtypeStruct((rows, D_GMLP), BF16),
            jax.ShapeDtypeStruct((n_latent_tiles // tiles_per_seq, radix, tiles_per_seq * per_class,
                                  D_FNET), BF16),
            jax.ShapeDtypeStruct(((n_tiles - n_latent_tiles) * tile, D_FNET), BF16),
        ],
        scratch_shapes=[pltpu.VMEM((D_FNET // LANES, tile, LANES), F32)],
        compiler_params=_cparams(("arbitrary",)),
        name="mix_in",
    )(h, mod, norm_g, w_in)


def _fft(xs):
    n = len(xs)
    if n == 1:
        return xs
    even, odd = _fft(xs[0::2]), _fft(xs[1::2])
    out = [None] * n
    for k in range(n // 2):
        (er, ei), (qr, qi) = even[k], odd[k]
        if k == 0:
            tr, ti = qr, qi
        elif 4 * k == n:
            tr, ti = qi, -qr
        else:
            wr, wi = math.cos(2 * math.pi * k / n), -math.sin(2 * math.pi * k / n)
            tr, ti = wr * qr - wi * qi, wr * qi + wi * qr
        out[k] = (er + tr, ei + ti)
        out[k + n // 2] = (er - tr, ei - ti)
    return out


def _fourier_kernel(x_ref, t_ref, cs_ref, o_ref, u_ref, p_ref, *, radix):
    q = DFT_Q
    n_half = D_FNET // LANES
    for a in range(radix):
        u = jnp.dot(t_ref[a], x_ref[a], preferred_element_type=F32)
        for part in range(2):
            for s in range(n_half):
                dst = p_ref if radix == 1 else u_ref.at[a]
                dst[part * n_half + s] = u[part * q:(part + 1) * q, s * LANES:(s + 1) * LANES]

    if radix > 1:
        def butterflies(i, _):
            s = i // (q // SUBLANES)
            r = pl.multiple_of((i % (q // SUBLANES)) * SUBLANES, SUBLANES)
            xs = [(u_ref[a, s, pl.ds(r, SUBLANES), :], u_ref[a, n_half + s, pl.ds(r, SUBLANES), :])
                  for a in range(radix)]
            for c, (pr, pi) in enumerate(_fft(xs)):
                p_ref[s, pl.ds(c * q + r, SUBLANES), :] = pr
                p_ref[n_half + s, pl.ds(c * q + r, SUBLANES), :] = pi
            return 0

        lax.fori_loop(0, n_half * (q // SUBLANES), butterflies, 0, unroll=2)

    for c in range(radix):
        rows = slice(c * q, (c + 1) * q)
        p = jnp.concatenate([p_ref[s, rows, :] for s in range(2 * n_half)], axis=1).astype(BF16)
        o_ref[rows, :] = jnp.dot(p, cs_ref[...], preferred_element_type=F32).astype(BF16)


def _fourier(f, tables, cs, *, batch, seq):
    radix = seq // DFT_Q
    n_slab = 2 * (D_FNET // LANES)
    return pl.pallas_call(
        functools.partial(_fourier_kernel, radix=radix),
        grid=(batch,),
        in_specs=[
            pl.BlockSpec((None, radix, DFT_Q, D_FNET), lambda b: (b, 0, 0, 0)),
            _resident((radix, 2 * DFT_Q, DFT_Q), lambda b: (0, 0, 0)),
            _resident((2 * D_FNET, D_FNET), lambda b: (0, 0)),
        ],
        out_specs=pl.BlockSpec((seq, D_FNET), lambda b: (b, 0)),
        out_shape=jax.ShapeDtypeStruct((batch * seq, D_FNET), BF16),
        scratch_shapes=[pltpu.VMEM((radix, n_slab, DFT_Q, LANES), F32),
                        pltpu.VMEM((n_slab, seq, LANES), F32)],
        compiler_params=_cparams(("arbitrary",)),
        name="fourier",
    )(f.reshape(batch, radix, DFT_Q, D_FNET), tables, cs)


def _position_dft_tables(seq):
    radix = seq // DFT_Q
    scale = 1.0 / math.sqrt(seq * HEAD_DIM)
    a = jnp.arange(radix, dtype=jnp.int32)[:, None]
    d = jnp.arange(DFT_Q, dtype=jnp.int32)
    ang_a = ((d[None, :] * a) % seq).astype(F32) * (2.0 * math.pi / seq)
    ang_b = ((d[:, None] * d[None, :]) % DFT_Q).astype(F32) * (2.0 * math.pi / DFT_Q)
    ca, sa = (jnp.cos(ang_a) * scale)[:, :, None], (jnp.sin(ang_a) * scale)[:, :, None]
    cb, sb = jnp.cos(ang_b)[None], jnp.sin(ang_b)[None]
    cos = ca * cb - sa * sb
    msin = -(sa * cb + ca * sb)
    return jnp.concatenate([cos, msin], axis=1).astype(BF16)


def _channel_dft_matrix():
    c = jnp.arange(D_FNET, dtype=jnp.int32)
    same_group = (c[:, None] // HEAD_DIM) == (c[None, :] // HEAD_DIM)
    ang = (((c[:, None] % HEAD_DIM) * (c[None, :] % HEAD_DIM)) % HEAD_DIM).astype(F32) * (
        2.0 * math.pi / HEAD_DIM)
    cs = jnp.concatenate([jnp.where(same_group, jnp.cos(ang), 0.0),
                          jnp.where(same_group, jnp.sin(ang), 0.0)], axis=0)
    return cs.astype(BF16)


def _pos_table_kernel(f_ref, o_ref):
    j = lax.broadcasted_iota(jnp.int32, (GRID_W, D_MODEL // 4), 0).astype(F32)
    e = j * f_ref[...]
    o_ref[...] = jnp.concatenate([jnp.sin(e), jnp.cos(e)], axis=1)


def _pos_table():
    q = D_MODEL // 4
    freqs = 1.0 / (10000.0 ** (jnp.arange(q, dtype=F32) / q))
    return pl.pallas_call(
        _pos_table_kernel,
        out_shape=jax.ShapeDtypeStruct((GRID_W, D_MODEL // 2), F32),
        name="pos_table",
    )(freqs.reshape(1, q))


def kernel(x, c, ctx, c_ctx, w_mod, b_mod, norm_g, ffn_w_gu, ffn_w_down, w_in, w_out, conv_w, conv_b,
           lru_w_gates, lru_b_gates, lru_lambda, gmlp_ws, gmlp_bs, final_norm_g):
    batch, seq, _ = x.shape
    ctx_len = ctx.shape[1]
    depth = w_mod.shape[0]
    assert batch < MOD_ROWS and seq % ROW_TILE == 0 and (batch * ctx_len) % ROW_TILE == 0
    assert seq % (N_SEG * LRU_CHUNK) == 0 and ctx_len % (N_SEG * SUBLANES) == 0
    assert LRU_CHUNK % (ctx_len // N_SEG) == 0 or (ctx_len // N_SEG) % LRU_CHUNK == 0
    radix = seq // DFT_Q
    assert seq == radix * DFT_Q and radix & (radix - 1) == 0 and ROW_TILE % radix == 0
    assert ctx_len == DFT_Q
    assert seq % MIX_IN_TILE == 0 and (batch * ctx_len) % MIX_IN_TILE == 0 and MIX_IN_TILE % radix == 0
    ctx_row = batch
    latent_rows = batch * seq
    tiles = dict(tiles_per_seq=seq // ROW_TILE)

    cvec = jnp.zeros((MOD_ROWS, D_MODEL), F32).at[:batch].set(c).at[ctx_row].set(c_ctx)
    mod = _adaln(cvec, w_mod, b_mod).reshape(depth, MOD_ROWS, N_MOD, D_MODEL)

    wgu = ffn_w_gu.astype(BF16)
    wd = ffn_w_down.astype(BF16)
    w_in_b = w_in.astype(BF16)
    w_out_b = w_out.astype(BF16)
    norm_g4 = norm_g.reshape(depth, 3, 1, D_MODEL)
    ws = gmlp_ws.astype(BF16).reshape(depth, N_GMLP_GROUPS * GMLP_CHUNK, GMLP_CHUNK)
    bs = jnp.repeat(jnp.swapaxes(gmlp_bs, 1, 2), HEAD_DIM, axis=2)
    wg = _lru_gate_weights(lru_w_gates, lru_b_gates)
    conv_b3 = conv_b.reshape(depth, 1, D_LRU)
    cs = _channel_dft_matrix()
    tab_lat = _position_dft_tables(seq)
    tab_ctx = _position_dft_tables(ctx_len)
    pos_tab = _pos_table()

    h = x.reshape(latent_rows, D_MODEL)
    zero_state = jnp.zeros((batch, 2, D_LRU), F32)

    for l in range(depth):
        last = l == depth - 1
        ffn = functools.partial(_ffn, mod=mod, norm_g=norm_g4, wgu=wgu, wd=wd, layer=l, **tiles)
        lru = functools.partial(_lru, conv_w=conv_w, conv_b=conv_b3, wg=wg, lam=lru_lambda,
                                layer=l, batch=batch)

        h = ffn(h, which=0,
                embed=(ctx.reshape(batch * ctx_len, D_MODEL), pos_tab) if l == 0 else None)
        xa, ga, gu, gv, f, f_c = _mix_in(h, mod, norm_g4, w_in_b, layer=l, radix=radix,
                                         tiles_per_seq=seq // MIX_IN_TILE,
                                         n_latent_tiles=latent_rows // MIX_IN_TILE)
        ya_c, state_c = lru(xa, ga, h0=zero_state, seq=ctx_len, first_row=latent_rows)
        ya, _ = lru(xa, ga, h0=state_c, seq=seq)
        yc = _fourier(f, tab_lat, cs, batch=batch, seq=seq)
        if last:
            ya_c = yc_c = None
        else:
            yc_c = _fourier(f_c, tab_ctx, cs, batch=batch, seq=ctx_len)
        h = ffn(h, which=1, mix=(ya, yc, ya_c, yc_c, gu, gv, ws, bs, w_out_b),
                final_g=final_norm_g if last else None,
                n_tiles=latent_rows // ROW_TILE if last else None)
    return h.reshape(batch, seq, D_MODEL)
```
